```python
import math
import jax, jax.numpy as jnp
from jax import lax
import numpy as np

D_MODEL = 1024
BATCH = 8
SEQ = 4096
DEPTH = 1

SSM_D_INNER = D_MODEL
SSM_HEAD_DIM = 64
SSM_HEADS = SSM_D_INNER // SSM_HEAD_DIM
SSM_GROUPS = 4
SSM_STATE = 128
SSM_CONV = 4
SSM_CHUNK = 128
SSM_CONV_DIM = SSM_D_INNER + 2 * SSM_GROUPS * SSM_STATE
ATT_D = D_MODEL
ATT_HEAD_DIM = 64
ATT_HEADS = ATT_D // ATT_HEAD_DIM
ATT_BLOCK = 128
RMS_EPS = 1e-6
IN_SPLITS = (SSM_D_INNER, SSM_CONV_DIM, SSM_HEADS, ATT_D, ATT_D, ATT_D, ATT_HEADS, ATT_D, D_MODEL, D_MODEL)
IN_COLS = SSM_D_INNER + SSM_CONV_DIM + SSM_HEADS + 4 * ATT_D + ATT_HEADS + 2 * D_MODEL

kernel_name = 'hybrid_ssd_fox_gated_merge'


def _offsets(sizes):
    out, acc = [], 0
    for s in sizes[:-1]:
        acc += s
        out.append(acc)
    return out


def _rmsnorm(x, g):
    xf = x.astype(jnp.float32)
    y = xf * lax.rsqrt(jnp.mean(xf * xf, axis=-1, keepdims=True) + RMS_EPS) * g.astype(jnp.float32)
    return y.astype(x.dtype)


def _causal_dwconv(u, w, b):
    y = lax.conv_general_dilated(u, w[:, None, :].astype(u.dtype), window_strides=(1,),
                                 padding=((SSM_CONV - 1, 0),),
                                 dimension_numbers=('NWC', 'WIO', 'NWC'),
                                 feature_group_count=u.shape[-1])
    return y + b.astype(u.dtype)


def _ssd_chunked(xs, dt, a, b_mat, c_mat, d_skip):
    bsz, seqlen, n_heads, p = xs.shape
    g, n = b_mat.shape[2], b_mat.shape[3]
    r = n_heads // g
    nc = seqlen // SSM_CHUNK
    q = SSM_CHUNK
    xdt = (xs * dt[..., None]).reshape(bsz, nc, q, g, r, p)
    bc = b_mat.reshape(bsz, nc, q, g, n)
    cc = c_mat.reshape(bsz, nc, q, g, n)
    da_cs = jnp.cumsum((dt * a).reshape(bsz, nc, q, g, r), axis=2)
    causal = jnp.tril(jnp.ones((q, q), dtype=bool))
    seg = da_cs[:, :, :, None] - da_cs[:, :, None, :]
    decay = jnp.exp(jnp.where(causal[None, None, :, :, None, None], seg, -jnp.inf))
    cb = jnp.einsum('bclgn,bcsgn->bclsg', cc, bc)
    y_diag = jnp.einsum('bclsgr,bcsgrp->bclgrp', cb[..., None] * decay, xdt)
    to_end = jnp.exp(da_cs[:, :, -1:] - da_cs)
    states = jnp.einsum('bcsgn,bcsgrp->bcgrpn', bc, xdt * to_end[..., None])
    chunk_decay = jnp.exp(da_cs[:, :, -1])

    def step(carry, inp):
        st, dec = inp
        return carry * dec[..., None, None] + st, carry

    init = jnp.zeros((bsz, g, r, p, n), states.dtype)
    _, prev = lax.scan(step, init, (jnp.moveaxis(states, 1, 0), jnp.moveaxis(chunk_decay, 1, 0)))
    prev = jnp.moveaxis(prev, 0, 1)
    y_off = jnp.einsum('bclgn,bcgrpn->bclgrp', cc, prev) * jnp.exp(da_cs)[..., None]
    y = (y_diag + y_off).reshape(bsz, seqlen, n_heads, p)
    return y + xs * d_skip[:, None]


def _forgetting_attention(q, k, v, log_f):
    seqlen = q.shape[2]
    scale = ATT_HEAD_DIM ** -0.5
    fcum = jnp.cumsum(log_f, axis=-1)
    outs = []
    for blk in range(seqlen // ATT_BLOCK):
        q0, q1 = blk * ATT_BLOCK, (blk + 1) * ATT_BLOCK
        s = jnp.einsum('bhqd,bhkd->bhqk', q[:, :, q0:q1], k[:, :, :q1]).astype(jnp.float32) * scale
        s = s + fcum[:, :, q0:q1, None] - fcum[:, :, None, :q1]
        qpos = q0 + jnp.arange(ATT_BLOCK)
        kpos = jnp.arange(q1)
        s = jnp.where(kpos[None, :] <= qpos[:, None], s, -jnp.inf)
        prob = jax.nn.softmax(s, axis=-1)
        outs.append(jnp.einsum('bhqk,bhkd->bhqd', prob.astype(v.dtype), v[:, :, :q1]))
    return jnp.concatenate(outs, axis=2)


def _hybrid_layer(x, pre_g, w_in, conv_w, conv_b, dt_bias, a_log, d_skip, ssm_norm_g,
                  fgate_b, w_branch_ssm, w_branch_att, w_out, post_g):
    bsz, seqlen, _ = x.shape
    f32 = jnp.float32
    h = _rmsnorm(x, pre_g)
    proj = jnp.einsum('bld,de->ble', h, w_in)
    z_ssm, xbc, dt_raw, q, k, v, f_raw, z_att, g_ssm, g_att = jnp.split(proj, _offsets(IN_SPLITS), axis=-1)

    xbc = jax.nn.silu(_causal_dwconv(xbc, conv_w, conv_b))
    gn = SSM_GROUPS * SSM_STATE
    xs, b_ssm, c_ssm = jnp.split(xbc, [SSM_D_INNER, SSM_D_INNER + gn], axis=-1)
    dt = jax.nn.softplus(dt_raw.astype(f32) + dt_bias.astype(f32))
    a = -jnp.exp(a_log.astype(f32))
    y = _ssd_chunked(xs.astype(f32).reshape(bsz, seqlen, SSM_HEADS, SSM_HEAD_DIM), dt, a,
                     b_ssm.astype(f32).reshape(bsz, seqlen, SSM_GROUPS, SSM_STATE),
                     c_ssm.astype(f32).reshape(bsz, seqlen, SSM_GROUPS, SSM_STATE),
                     d_skip.astype(f32))
    yg = (y.reshape(bsz, seqlen, SSM_D_INNER) * jax.nn.silu(z_ssm.astype(f32)))
    yg = yg.reshape(bsz, seqlen, SSM_GROUPS, SSM_D_INNER // SSM_GROUPS)
    yg = yg * lax.rsqrt(jnp.mean(yg * yg, axis=-1, keepdims=True) + RMS_EPS)
    y_ssm = (yg.reshape(bsz, seqlen, SSM_D_INNER) * ssm_norm_g.astype(f32)).astype(x.dtype)

    def heads(t):
        return t.reshape(bsz, seqlen, ATT_HEADS, ATT_HEAD_DIM).transpose(0, 2, 1, 3)
    log_f = jax.nn.log_sigmoid(f_raw.astype(f32) + fgate_b.astype(f32)).transpose(0, 2, 1)
    o = _forgetting_attention(heads(q), heads(k), heads(v), log_f)
    o = o.transpose(0, 2, 1, 3).reshape(bsz, seqlen, ATT_D)
    y_att = (o.astype(f32) * jax.nn.silu(z_att.astype(f32))).astype(x.dtype)

    p_ssm = jnp.einsum('ble,ed->bld', y_ssm, w_branch_ssm)
    p_att = jnp.einsum('ble,ed->bld', y_att, w_branch_att)
    merged = jax.nn.sigmoid(g_ssm) * p_ssm + jax.nn.sigmoid(g_att) * p_att
    out = jnp.einsum('bld,de->ble', merged, w_out)
    return x + _rmsnorm(out, post_g)


def setup_inputs(seed: int = 0) -> dict:
    key = jax.random.key(seed)
    ks = jax.random.split(key, 16)
    f32 = jnp.float32
    nl = DEPTH
    nrm = jax.random.normal
    x = nrm(ks[0], (BATCH, SEQ, D_MODEL), f32)
    pre_norm_g = 1.0 + 0.05 * nrm(ks[1], (nl, D_MODEL), f32)
    w_in = nrm(ks[2], (nl, D_MODEL, IN_COLS), f32) * D_MODEL ** -0.5
    conv_w = nrm(ks[3], (nl, SSM_CONV, SSM_CONV_DIM), f32) * SSM_CONV ** -0.5
    conv_b = 0.02 * nrm(ks[4], (nl, SSM_CONV_DIM), f32)
    dt0 = jnp.exp(jax.random.uniform(ks[5], (nl, SSM_HEADS), f32, minval=math.log(1e-3), maxval=math.log(1e-1)))
    dt_bias = dt0 + jnp.log(-jnp.expm1(-dt0))
    a_log = jnp.log(jax.random.uniform(ks[6], (nl, SSM_HEADS), f32, minval=1.0, maxval=16.0))
    d_skip = 1.0 + 0.1 * nrm(ks[7], (nl, SSM_HEADS), f32)
    ssm_norm_g = 1.0 + 0.05 * nrm(ks[8], (nl, SSM_D_INNER), f32)
    fgate_b = jax.random.uniform(ks[9], (nl, ATT_HEADS), f32, minval=1.0, maxval=4.0)
    w_branch_ssm = nrm(ks[10], (nl, SSM_D_INNER, D_MODEL), f32) * SSM_D_INNER ** -0.5
    w_branch_att = nrm(ks[11], (nl, ATT_D, D_MODEL), f32) * ATT_D ** -0.5
    w_out = nrm(ks[12], (nl, D_MODEL, D_MODEL), f32) * D_MODEL ** -0.5
    post_norm_g = 1.0 + 0.05 * nrm(ks[13], (nl, D_MODEL), f32)
    return {'x': x, 'pre_norm_g': pre_norm_g, 'w_in': w_in, 'conv_w': conv_w, 'conv_b': conv_b,
            'dt_bias': dt_bias, 'a_log': a_log, 'd_skip': d_skip, 'ssm_norm_g': ssm_norm_g,
            'fgate_b': fgate_b, 'w_branch_ssm': w_branch_ssm, 'w_branch_att': w_branch_att,
            'w_out': w_out, 'post_norm_g': post_norm_g}


def reference(x, pre_norm_g, w_in, conv_w, conv_b, dt_bias, a_log, d_skip, ssm_norm_g,
              fgate_b, w_branch_ssm, w_branch_att, w_out, post_norm_g):
    for i in range(DEPTH):
        x = _hybrid_layer(x, pre_norm_g[i], w_in[i], conv_w[i], conv_b[i], dt_bias[i], a_log[i],
                          d_skip[i], ssm_norm_g[i], fgate_b[i], w_branch_ssm[i], w_branch_att[i],
                          w_out[i], post_norm_g[i])
    return x
```

```python
import functools

import jax
import jax.numpy as jnp
from jax import lax
from jax.experimental import pallas as pl
from jax.experimental.pallas import tpu as pltpu

F32 = jnp.float32
BF16 = jnp.bfloat16

RMS_EPS = 1e-6
HEAD_DIM = 64
SSM_GROUPS = 4
SSM_STATE = 128
SSM_CONV = 4
SSM_CHUNK = 128
LANES = 128
VMEM_LIMIT = 48 * 1024 * 1024

COL_Z_SSM, COL_XS, COL_BC, COL_Q, COL_K, COL_V, COL_Z_ATT, COL_G_SSM, COL_G_ATT = range(9)
N_COL_BLOCKS = 9


def _sigmoid(x):
    return 1.0 / (1.0 + jnp.exp(-x))


def _silu(x):
    return x * _sigmoid(x)


def _softplus(x):
    return jnp.maximum(x, 0.0) + jnp.log(1.0 + jnp.exp(-jnp.abs(x)))


def _split3(x):
    hi = x.astype(BF16)
    r1 = x - hi.astype(F32)
    mid = r1.astype(BF16)
    lo = (r1 - mid.astype(F32)).astype(BF16)
    return hi, mid, lo


def _tri_cumsum(tri_bf16, x):
    hi, mid, lo = _split3(x)
    dot = functools.partial(jnp.dot, preferred_element_type=F32)
    return dot(tri_bf16, hi) + dot(tri_bf16, mid) + dot(tri_bf16, lo)


def _lower_tri(n):
    r = lax.broadcasted_iota(jnp.int32, (n, n), 0)
    c = lax.broadcasted_iota(jnp.int32, (n, n), 1)
    return r >= c


def _in_proj_kernel(x_ref, g_ref, w_ref, ws_ref, p_ref, s_ref, h_ref):
    j = pl.program_id(1)

    @pl.when(j == 0)
    def _():
        x = x_ref[...]
        ms = jnp.mean(x * x, axis=-1, keepdims=True)
        h = (x * lax.rsqrt(ms + RMS_EPS) * g_ref[...]).astype(BF16)
        h_ref[...] = h
        s_ref[...] = jnp.dot(h, ws_ref[...], preferred_element_type=F32)

    p_ref[...] = jnp.dot(h_ref[...], w_ref[...], preferred_element_type=F32).astype(BF16)


def _in_proj(x2, g, w_big, w_small, tm):
    m, d = x2.shape
    return pl.pallas_call(
        _in_proj_kernel,
        grid=(m // tm, N_COL_BLOCKS),
        in_specs=[
            pl.BlockSpec((tm, d), lambda i, j: (i, 0)),
            pl.BlockSpec((1, d), lambda i, j: (0, 0)),
            pl.BlockSpec((d, d), lambda i, j: (0, j)),
            pl.BlockSpec((d, LANES), lambda i, j: (0, 0)),
        ],
        out_specs=[
            pl.BlockSpec((tm, d), lambda i, j: (i, j)),
            pl.BlockSpec((tm, LANES), lambda i, j: (i, 0)),
        ],
        out_shape=[
            jax.ShapeDtypeStruct((m, N_COL_BLOCKS * d), BF16),
            jax.ShapeDtypeStruct((m, LANES), F32),
        ],
        scratch_shapes=[pltpu.VMEM((tm, d), BF16)],
        compiler_params=pltpu.CompilerParams(
            dimension_semantics=("arbitrary", "arbitrary"), vmem_limit_bytes=VMEM_LIMIT),
        name="in_proj",
    )(x2, g, w_big, w_small)


def _fcumsum_kernel(s_ref, b_ref, f_ref, *, n_heads):
    seqlen = s_ref.shape[0]
    blk = SSM_CHUNK
    tri = _lower_tri(blk).astype(BF16)
    lane = lax.broadcasted_iota(jnp.int32, (blk, LANES), 1)
    valid = (lane >= n_heads) & (lane < 2 * n_heads)

    def body(c, carry):
        r0 = pl.multiple_of(c * blk, blk)
        raw = s_ref[pl.ds(r0, blk), :] + b_ref[...]
        logf = jnp.where(valid, -_softplus(-raw), 0.0)
        f_blk = _tri_cumsum(tri, logf) + carry
        f_ref[pl.ds(r0, blk), :] = f_blk
        return f_blk[blk - 1:blk, :]

    lax.fori_loop(0, seqlen // blk, body, jnp.zeros((1, LANES), F32))


def _fcumsum(small, fb_vec, bsz, seqlen, n_heads):
    return pl.pallas_call(
        functools.partial(_fcumsum_kernel, n_heads=n_heads),
        grid=(bsz,),
        in_specs=[
            pl.BlockSpec((seqlen, LANES), lambda b: (b, 0)),
            pl.BlockSpec((1, LANES), lambda b: (0, 0)),
        ],
        out_specs=pl.BlockSpec((seqlen, LANES), lambda b: (b, 0)),
        out_shape=jax.ShapeDtypeStruct((bsz * seqlen, LANES), F32),
        compiler_params=pltpu.CompilerParams(
            dimension_semantics=("arbitrary",), vmem_limit_bytes=VMEM_LIMIT),
        name="fcumsum",
    )(small, fb_vec)


def _bias_pieces(f_blk, lane_a):
    rows = f_blk.shape[0]
    lane = lax.broadcasted_iota(jnp.int32, (rows, LANES), 1)
    fa = jnp.sum(jnp.where(lane == lane_a, f_blk, 0.0), axis=1, keepdims=True)
    fb = jnp.sum(jnp.where(lane == lane_a + 1, f_blk, 0.0), axis=1, keepdims=True)
    x = jnp.where(lane < 8, fa, fb)
    hi, mid, lo = _split3(x)
    return lane, hi.astype(F32), mid.astype(F32), lo.astype(F32)


def _attn_kernel(q_ref, k_ref, v_ref, f_ref, z_ref, o_ref, kaug_ref, *, n_heads, tq):
    hp = pl.program_id(1)
    i = pl.program_id(2)
    lane_a = n_heads + 2 * hp

    @pl.when(i == 0)
    def _():
        lane, hi, mid, lo = _bias_pieces(f_ref[...], lane_a)
        r = lane % 8
        aug = jnp.where(r < 3, 1.0, jnp.where(r == 3, -hi, jnp.where(r == 4, -mid, jnp.where(r == 5, -lo, 0.0))))
        aug = jnp.where(lane < 16, aug, 0.0)
        kaug_ref[:, :LANES] = k_ref[...]
        kaug_ref[:, LANES:] = aug.astype(BF16)

    q0 = pl.multiple_of(i * tq, tq)
    lane, hi, mid, lo = _bias_pieces(f_ref[pl.ds(q0, tq), :], lane_a)
    r = lane % 8
    aug = jnp.where(r == 0, hi, jnp.where(r == 1, mid, jnp.where(r == 2, lo, jnp.where(r < 6, 1.0, 0.0))))
    q = q_ref[...].astype(F32)
    q_a = jnp.concatenate([jnp.where(lane < HEAD_DIM, q, 0.0), jnp.where(lane < 8, aug, 0.0)], axis=1)
    q_b = jnp.concatenate([jnp.where(lane >= HEAD_DIM, q, 0.0),
                           jnp.where((lane >= 8) & (lane < 16), aug, 0.0)], axis=1)
    q_ab = jnp.concatenate([q_a, q_b], axis=0).astype(BF16)

    lane_o = lax.broadcasted_iota(jnp.int32, (tq, LANES), 1)
    is_a = lane_o < HEAD_DIM

    def scores(k0):
        k_tile = kaug_ref[pl.ds(k0, tq), :]
        return lax.dot_general(q_ab, k_tile, (((1,), (1,)), ((), ())), preferred_element_type=F32)

    def pv(p, k0):
        v_tile = v_ref[pl.ds(k0, tq), :]
        pv_a = jnp.dot(p[:tq].astype(BF16), v_tile, preferred_element_type=F32)
        pv_b = jnp.dot(p[tq:].astype(BF16), v_tile, preferred_element_type=F32)
        return jnp.where(is_a, pv_a, pv_b)

    s = scores(q0)
    causal = _lower_tri(tq)
    s = jnp.where(jnp.concatenate([causal, causal], axis=0), s, -jnp.inf)
    m = jnp.max(s, axis=1, keepdims=True)
    p = jnp.exp(s - m)
    l = jnp.sum(p, axis=1, keepdims=True)
    acc = pv(p, q0)

    def body(j, carry):
        m, l, acc = carry
        k0 = pl.multiple_of(j * tq, tq)
        s = scores(k0)
        m_new = jnp.maximum(m, jnp.max(s, axis=1, keepdims=True))
        alpha = jnp.exp(m - m_new)
        p = jnp.exp(s - m_new)
        l = alpha * l + jnp.sum(p, axis=1, keepdims=True)
        acc = acc * jnp.where(is_a, alpha[:tq], alpha[tq:]) + pv(p, k0)
        return m_new, l, acc

    m, l, acc = lax.fori_loop(0, i, body, (m, l, acc))
    o = acc / jnp.where(is_a, l[:tq], l[tq:])
    o_ref[...] = (o * _silu(z_ref[...].astype(F32))).astype(BF16)


def _attention(proj, fcum, bsz, seqlen, d_model, n_heads, tq):
    n_hp = n_heads // 2
    per_row = d_model // LANES
    nq = seqlen // tq
    return pl.pallas_call(
        functools.partial(_attn_kernel, n_heads=n_heads, tq=tq),
        grid=(bsz, n_hp, nq),
        in_specs=[
            pl.BlockSpec((tq, LANES), lambda b, h, i: (b * nq + i, COL_Q * per_row + h)),
            pl.BlockSpec((seqlen, LANES), lambda b, h, i: (b, COL_K * per_row + h)),
            pl.BlockSpec((seqlen, LANES), lambda b, h, i: (b, COL_V * per_row + h)),
            pl.BlockSpec((seqlen, LANES), lambda b, h, i: (b, 0)),
            pl.BlockSpec((tq, LANES), lambda b, h, i: (b * nq + i, COL_Z_ATT * per_row + h)),
        ],
        out_specs=pl.BlockSpec((tq, LANES), lambda b, h, i: (b * nq + i, h)),
        out_shape=jax.ShapeDtypeStruct((bsz * seqlen, d_model), BF16),
        scratch_shapes=[pltpu.VMEM((seqlen, 2 * LANES), BF16)],
        compiler_params=pltpu.CompilerParams(
            dimension_semantics=("arbitrary", "arbitrary", "arbitrary"), vmem_limit_bytes=VMEM_LIMIT),
        name="fox_attention",
    )(proj, proj, proj, fcum, proj)


def _ssd_kernel(xs_ref, bc_ref, z_ref, s_ref, cw_ref, cb_ref, dtb_ref, alog_ref, dskip_ref, ng_ref,
                expand_ref, y_ref, buf_ref, st_ref, *, n_heads):
    c = pl.program_id(1)
    q = SSM_CHUNK
    d_inner = xs_ref.shape[1]
    gn = SSM_GROUPS * SSM_STATE
    heads_per_group = n_heads // SSM_GROUPS
    gw = heads_per_group * HEAD_DIM
    hdr = 8

    @pl.when(c == 0)
    def _():
        buf_ref[:hdr, :] = jnp.zeros((hdr, buf_ref.shape[1]), F32)
        st_ref[...] = jnp.zeros(st_ref.shape, F32)

    buf_ref[hdr:, :d_inner] = xs_ref[...].astype(F32)
    buf_ref[hdr:, d_inner:] = bc_ref[...].astype(F32)
    conv = cb_ref[...] + cw_ref[SSM_CONV - 1:SSM_CONV, :] * buf_ref[hdr:, :]
    for jj in range(SSM_CONV - 1):
        off = hdr - (SSM_CONV - 1) + jj
        conv = conv + cw_ref[jj:jj + 1, :] * buf_ref[off:off + q, :]
    tail = buf_ref[q:, :]
    buf_ref[:hdr, :] = tail
    act = _silu(conv)
    xs = act[:, :d_inner]
    b16 = act[:, d_inner:d_inner + gn].astype(BF16)
    c16 = act[:, d_inner + gn:].astype(BF16)

    lane = lax.broadcasted_iota(jnp.int32, (q, LANES), 1)
    head_lane = lane < n_heads
    dt = jnp.where(head_lane, _softplus(s_ref[...] + dtb_ref[...]), 0.0)
    da = dt * (-jnp.exp(alog_ref[...]))
    tri = _lower_tri(q)
    cs = _tri_cumsum(tri.astype(BF16), da)
    cs_t = cs.T
    dt_t = dt.T
    last = cs[q - 1:q, :]
    w_end = dt * jnp.exp(last - cs)
    ecs = jnp.exp(cs)

    def expand(v):
        hi, mid, _ = _split3(v)
        e = expand_ref[...]
        return jnp.dot(hi, e, preferred_element_type=F32) + jnp.dot(mid, e, preferred_element_type=F32)

    w_e = expand(w_end)
    ecs_e = expand(ecs)
    xw16 = (xs * w_e).astype(BF16)

    first_half = (lax.broadcasted_iota(jnp.int32, (q, d_inner), 1) % LANES) < HEAD_DIM
    xs_first16 = jnp.where(first_half, xs, 0.0).astype(BF16)
    xs_second16 = jnp.where(first_half, 0.0, xs).astype(BF16)

    y_parts = []
    for g in range(SSM_GROUPS):
        bg = b16[:, g * SSM_STATE:(g + 1) * SSM_STATE]
        cg = c16[:, g * SSM_STATE:(g + 1) * SSM_STATE]
        cb = lax.dot_general(cg, bg, (((1,), (1,)), ((), ())), preferred_element_type=F32)
        st_prev = st_ref[g]
        y_off = jnp.dot(cg, st_prev.astype(BF16), preferred_element_type=F32)
        for pair in range(heads_per_group // 2):
            lo_lane = g * gw + pair * LANES
            y_pair = None
            for half in range(2):
                h = g * heads_per_group + pair * 2 + half
                seg = cs[:, h:h + 1] - cs_t[h:h + 1, :]
                decay = jnp.exp(jnp.where(tri, seg, -jnp.inf))
                mmat = (cb * decay * dt_t[h:h + 1, :]).astype(BF16)
                x_half = (xs_first16 if half == 0 else xs_second16)[:, lo_lane:lo_lane + LANES]
                contrib = jnp.dot(mmat, x_half, preferred_element_type=F32)
                y_pair = contrib if y_pair is None else y_pair + contrib
            y_pair = y_pair + y_off[:, pair * LANES:(pair + 1) * LANES] * ecs_e[:, lo_lane:lo_lane + LANES]
            y_parts.append(y_pair)
        bg_t = bg.astype(F32).T.astype(BF16)
        new_st = jnp.dot(bg_t, xw16[:, g * gw:(g + 1) * gw], preferred_element_type=F32)
        st_ref[g] = st_prev * ecs_e[q - 1:q, g * gw:(g + 1) * gw] + new_st

    y = jnp.concatenate(y_parts, axis=1) + xs * dskip_ref[...]
    yg = y * _silu(z_ref[...].astype(F32))
    outs = []
    for g in range(SSM_GROUPS):
        blk = yg[:, g * gw:(g + 1) * gw]
        ms = jnp.mean(blk * blk, axis=-1, keepdims=True)
        outs.append(blk * lax.rsqrt(ms + RMS_EPS))
    y_ref[...] = (jnp.concatenate(outs, axis=1) * ng_ref[...]).astype(BF16)


def _ssd(proj, small, conv_w, conv_b, dtb_vec, alog_vec, dskip_e, norm_g, expand_m, bsz, seqlen, d_model, n_heads):
    nc = seqlen // SSM_CHUNK
    q = SSM_CHUNK
    conv_dim = conv_w.shape[1]
    gw = (n_heads // SSM_GROUPS) * HEAD_DIM
    return pl.pallas_call(
        functools.partial(_ssd_kernel, n_heads=n_heads),
        grid=(bsz, nc),
        in_specs=[
            pl.BlockSpec((q, d_model), lambda b, c: (b * nc + c, COL_XS)),
            pl.BlockSpec((q, d_model), lambda b, c: (b * nc + c, COL_BC)),
            pl.BlockSpec((q, d_model), lambda b, c: (b * nc + c, COL_Z_SSM)),
            pl.BlockSpec((q, LANES), lambda b, c: (b * nc + c, 0)),
            pl.BlockSpec((SSM_CONV, conv_dim), lambda b, c: (0, 0)),
            pl.BlockSpec((1, conv_dim), lambda b, c: (0, 0)),
            pl.BlockSpec((1, LANES), lambda b, c: (0, 0)),
            pl.BlockSpec((1, LANES), lambda b, c: (0, 0)),
            pl.BlockSpec((1, d_model), lambda b, c: (0, 0)),
            pl.BlockSpec((1, d_model), lambda b, c: (0, 0)),
            pl.BlockSpec((LANES, d_model), lambda b, c: (0, 0)),
        ],
        out_specs=pl.BlockSpec((q, d_model), lambda b, c: (b * nc + c, 0)),
        out_shape=jax.ShapeDtypeStruct((bsz * seqlen, d_model), BF16),
        scratch_shapes=[
            pltpu.VMEM((q + 8, conv_dim), F32),
            pltpu.VMEM((SSM_GROUPS, SSM_STATE, gw), F32),
        ],
        compiler_params=pltpu.CompilerParams(
            dimension_semantics=("arbitrary", "arbitrary"), vmem_limit_bytes=VMEM_LIMIT),
        name="ssd_branch",
    )(proj, proj, proj, small, conv_w, conv_b, dtb_vec, alog_vec, dskip_e, norm_g, expand_m)


def _merge_kernel(ys_ref, ya_ref, gs_ref, ga_ref, x_ref, ws_ref, wa_ref, wo_ref, pg_ref, o_ref):
    p_ssm = jnp.dot(ys_ref[...], ws_ref[...], preferred_element_type=F32)
    p_att = jnp.dot(ya_ref[...], wa_ref[...], preferred_element_type=F32)
    merged = _sigmoid(gs_ref[...].astype(F32)) * p_ssm + _sigmoid(ga_ref[...].astype(F32)) * p_att
    out = jnp.dot(merged.astype(BF16), wo_ref[...], preferred_element_type=F32)
    ms = jnp.mean(out * out, axis=-1, keepdims=True)
    o_ref[...] = x_ref[...] + out * lax.rsqrt(ms + RMS_EPS) * pg_ref[...]


def _merge(y_ssm, y_att, proj, x2, w_ssm, w_att, w_out, post_g, tm):
    m, d = x2.shape
    row = lambda i: (i, 0)
    full = lambda i: (0, 0)
    return pl.pallas_call(
        _merge_kernel,
        grid=(m // tm,),
        in_specs=[
            pl.BlockSpec((tm, d), row),
            pl.BlockSpec((tm, d), row),
            pl.BlockSpec((tm, d), lambda i: (i, COL_G_SSM)),
            pl.BlockSpec((tm, d), lambda i: (i, COL_G_ATT)),
            pl.BlockSpec((tm, d), row),
            pl.BlockSpec((d, d), full),
            pl.BlockSpec((d, d), full),
            pl.BlockSpec((d, d), full),
            pl.BlockSpec((1, d), full),
        ],
        out_specs=pl.BlockSpec((tm, d), row),
        out_shape=jax.ShapeDtypeStruct((m, d), F32),
        compiler_params=pltpu.CompilerParams(
            dimension_semantics=("arbitrary",), vmem_limit_bytes=VMEM_LIMIT),
        name="gated_merge",
    )(y_ssm, y_att, proj, proj, x2, w_ssm, w_att, w_out, post_g)


def _pad_lanes(v, offset):
    out = jnp.zeros((1, LANES), F32)
    return lax.dynamic_update_slice(out, v.reshape(1, -1).astype(F32), (0, offset))


def _layer(x, pre_g, w_in, conv_w, conv_b, dt_bias, a_log, d_skip, ssm_norm_g, fgate_b,
           w_branch_ssm, w_branch_att, w_out, post_g):
    bsz, seqlen, d = x.shape
    n_heads = d // HEAD_DIM
    gn = SSM_GROUPS * SSM_STATE
    sizes = (d, d + 2 * gn, n_heads, d, d, d, n_heads, d, d, d)
    offs = [0]
    for s in sizes:
        offs.append(offs[-1] + s)
    z_ssm, xbc, dt_w, q_w, k_w, v_w, f_w, z_att, g_ssm, g_att = (w_in[:, offs[n]:offs[n + 1]] for n in range(10))
    scale = HEAD_DIM ** -0.5
    w_big = jnp.concatenate([z_ssm, xbc, q_w * scale, k_w, v_w, z_att, g_ssm, g_att], axis=1).astype(BF16)
    w_small = jnp.concatenate([dt_w, f_w, jnp.zeros((d, LANES - 2 * n_heads), w_in.dtype)], axis=1).astype(BF16)

    x2 = x.reshape(bsz * seqlen, d)
    proj, small = _in_proj(x2, pre_g.reshape(1, d), w_big, w_small, tm=min(1024, bsz * seqlen))

    fcum = _fcumsum(small, _pad_lanes(fgate_b, n_heads), bsz, seqlen, n_heads)
    y_att = _attention(proj, fcum, bsz, seqlen, d, n_heads, tq=min(256, seqlen))

    head_of_col = jnp.arange(d) // HEAD_DIM
    expand_m = (jnp.arange(LANES)[:, None] == head_of_col[None, :]).astype(BF16)
    y_ssm = _ssd(proj, small, conv_w, conv_b.reshape(1, -1), _pad_lanes(dt_bias, 0), _pad_lanes(a_log, 0),
                 jnp.repeat(d_skip.astype(F32), HEAD_DIM).reshape(1, d), ssm_norm_g.reshape(1, d), expand_m,
                 bsz, seqlen, d, n_heads)

    out = _merge(y_ssm, y_att, proj, x2, w_branch_ssm.astype(BF16), w_branch_att.astype(BF16),
                 w_out.astype(BF16), post_g.reshape(1, d), tm=min(512, bsz * seqlen))
    return out.reshape(bsz, seqlen, d)


def kernel(x, pre_norm_g, w_in, conv_w, conv_b, dt_bias, a_log, d_skip, ssm_norm_g, fgate_b,
           w_branch_ssm, w_branch_att, w_out, post_norm_g):
    for i in range(pre_norm_g.shape[0]):
        x = _layer(x, pre_norm_g[i], w_in[i], conv_w[i], conv_b[i], dt_bias[i], a_log[i], d_skip[i],
                   ssm_norm_g[i], fgate_b[i], w_branch_ssm[i], w_branch_att[i], w_out[i], post_norm_g[i])
    return x
```

```python
import functools

import jax
import jax.numpy as jnp
from jax import lax
from jax.experimental import pallas as pl
from jax.experimental.pallas import tpu as pltpu

F32 = jnp.float32
BF16 = jnp.bfloat16

RMS_EPS = 1e-6
HEAD_DIM = 64
SSM_GROUPS = 4
SSM_STATE = 128
SSM_CONV = 4
SSM_CHUNK = 128
KV_SUB = 256
LANES = 128
VMEM_LIMIT = 48 * 1024 * 1024

COL_Z_SSM, COL_XS, COL_BC, COL_Q, COL_K, COL_V, COL_Z_ATT, COL_G_SSM, COL_G_ATT = range(9)
N_COL_BLOCKS = 9


def _sigmoid(x):
    return 1.0 / (1.0 + jnp.exp(-x))


def _silu(x):
    return x * _sigmoid(x)


def _softplus(x):
    return jnp.maximum(x, 0.0) + jnp.log(1.0 + jnp.exp(-jnp.abs(x)))


def _split3(x):
    hi = x.astype(BF16)
    r1 = x - hi.astype(F32)
    mid = r1.astype(BF16)
    lo = (r1 - mid.astype(F32)).astype(BF16)
    return hi, mid, lo


def _tri_cumsum(tri_bf16, x):
    hi, mid, lo = _split3(x)
    dot = functools.partial(jnp.dot, preferred_element_type=F32)
    return dot(tri_bf16, hi) + dot(tri_bf16, mid) + dot(tri_bf16, lo)


def _lower_tri(n):
    r = lax.broadcasted_iota(jnp.int32, (n, n), 0)
    c = lax.broadcasted_iota(jnp.int32, (n, n), 1)
    return r >= c


def _in_proj_kernel(x_ref, g_ref, w_ref, ws_ref, p_ref, s_ref, h_ref):
    j = pl.program_id(1)

    @pl.when(j == 0)
    def _():
        x = x_ref[...]
        ms = jnp.mean(x * x, axis=-1, keepdims=True)
        h = (x * lax.rsqrt(ms + RMS_EPS) * g_ref[...]).astype(BF16)
        h_ref[...] = h
        s_ref[...] = jnp.dot(h, ws_ref[...], preferred_element_type=F32)

    p_ref[...] = jnp.dot(h_ref[...], w_ref[...], preferred_element_type=F32).astype(BF16)


def _in_proj(x2, g, w_big, w_small, tm):
    m, d = x2.shape
    return pl.pallas_call(
        _in_proj_kernel,
        grid=(m // tm, N_COL_BLOCKS),
        in_specs=[
            pl.BlockSpec((tm, d), lambda i, j: (i, 0)),
            pl.BlockSpec((1, d), lambda i, j: (0, 0)),
            pl.BlockSpec((d, d), lambda i, j: (0, j)),
            pl.BlockSpec((d, LANES), lambda i, j: (0, 0)),
        ],
        out_specs=[
            pl.BlockSpec((tm, d), lambda i, j: (i, j)),
            pl.BlockSpec((tm, LANES), lambda i, j: (i, 0)),
        ],
        out_shape=[
            jax.ShapeDtypeStruct((m, N_COL_BLOCKS * d), BF16),
            jax.ShapeDtypeStruct((m, LANES), F32),
        ],
        scratch_shapes=[pltpu.VMEM((tm, d), BF16)],
        compiler_params=pltpu.CompilerParams(
            dimension_semantics=("arbitrary", "arbitrary"), vmem_limit_bytes=VMEM_LIMIT),
        name="in_proj",
    )(x2, g, w_big, w_small)


def _fcumsum_kernel(s_ref, b_ref, f_ref, *, n_heads):
    seqlen = s_ref.shape[0]
    blk = SSM_CHUNK
    tri = _lower_tri(blk).astype(BF16)
    lane = lax.broadcasted_iota(jnp.int32, (blk, LANES), 1)
    valid = (lane >= n_heads) & (lane < 2 * n_heads)

    def body(c, carry):
        r0 = pl.multiple_of(c * blk, blk)
        raw = s_ref[pl.ds(r0, blk), :] + b_ref[...]
        logf = jnp.where(valid, -_softplus(-raw), 0.0)
        f_blk = _tri_cumsum(tri, logf) + carry
        f_ref[pl.ds(r0, blk), :] = f_blk
        return f_blk[blk - 1:blk, :]

    lax.fori_loop(0, seqlen // blk, body, jnp.zeros((1, LANES), F32))


def _fcumsum(small, fb_vec, bsz, seqlen, n_heads):
    return pl.pallas_call(
        functools.partial(_fcumsum_kernel, n_heads=n_heads),
        grid=(bsz,),
        in_specs=[
            pl.BlockSpec((seqlen, LANES), lambda b: (b, 0)),
            pl.BlockSpec((1, LANES), lambda b: (0, 0)),
        ],
        out_specs=pl.BlockSpec((seqlen, LANES), lambda b: (b, 0)),
        out_shape=jax.ShapeDtypeStruct((bsz * seqlen, LANES), F32),
        compiler_params=pltpu.CompilerParams(
            dimension_semantics=("arbitrary",), vmem_limit_bytes=VMEM_LIMIT),
        name="fcumsum",
    )(small, fb_vec)


def _bias_pieces(f_blk, lane_a):
    rows = f_blk.shape[0]
    lane = lax.broadcasted_iota(jnp.int32, (rows, LANES), 1)
    fa = jnp.sum(jnp.where(lane == lane_a, f_blk, 0.0), axis=1, keepdims=True)
    fb = jnp.sum(jnp.where(lane == lane_a + 1, f_blk, 0.0), axis=1, keepdims=True)
    x = jnp.where(lane < 8, fa, fb)
    hi, mid, lo = _split3(x)
    return lane, hi.astype(F32), mid.astype(F32), lo.astype(F32)


def _attn_kernel(q_ref, k_ref, v_ref, f_ref, z_ref, o_ref, kaug_ref, vta_ref, vtb_ref, *, n_heads, tq):
    hp = pl.program_id(1)
    i = pl.program_id(2)
    lane_a = n_heads + 2 * hp
    tk = KV_SUB
    sub_per_q = tq // tk
    n_kv = k_ref.shape[0] // tk

    @pl.when(i == 0)
    def _():
        lane, hi, mid, lo = _bias_pieces(f_ref[...], lane_a)
        r = lane % 8
        aug = jnp.where(r < 3, 1.0, jnp.where(r == 3, -hi, jnp.where(r == 4, -mid, jnp.where(r == 5, -lo, 0.0))))
        aug = jnp.where(lane < 16, aug, 0.0)
        kaug_ref[:, :LANES] = k_ref[...]
        kaug_ref[:, LANES:] = aug.astype(BF16)

        row = lax.broadcasted_iota(jnp.int32, (LANES, tk), 0)

        def transpose_tile(c, carry):
            r0 = pl.multiple_of(c * tk, tk)
            vt = v_ref[pl.ds(r0, tk), :].astype(F32).T
            vta_ref[c] = jnp.where(row < HEAD_DIM, vt, 1.0).astype(BF16)
            vtb_ref[c] = jnp.where(row < HEAD_DIM, 1.0, vt).astype(BF16)
            return carry

        lax.fori_loop(0, n_kv, transpose_tile, 0)

    q0 = pl.multiple_of(i * tq, tq)
    lane, hi, mid, lo = _bias_pieces(f_ref[pl.ds(q0, tq), :], lane_a)
    r = lane % 8
    aug = jnp.where(r == 0, hi, jnp.where(r == 1, mid, jnp.where(r == 2, lo, jnp.where(r < 6, 1.0, 0.0))))
    q = q_ref[...].astype(F32)
    qt_a = jnp.concatenate([jnp.where(lane < HEAD_DIM, q, 0.0), jnp.where(lane < 8, aug, 0.0)],
                           axis=1).T.astype(BF16)
    qt_b = jnp.concatenate([jnp.where(lane >= HEAD_DIM, q, 0.0), jnp.where((lane >= 8) & (lane < 16), aug, 0.0)],
                           axis=1).T.astype(BF16)

    def update(m, acc, s, vt):
        mn = jnp.maximum(m, jnp.max(s, axis=0, keepdims=True))
        p = jnp.exp(s - mn).astype(BF16)
        return mn, acc * jnp.exp(m - mn) + jnp.dot(vt, p, preferred_element_type=F32)

    def scores(sub, n0):
        k_sub = kaug_ref[pl.ds(pl.multiple_of(sub * tk, tk), tk), :]
        return (jnp.dot(k_sub, qt_a[:, n0:], preferred_element_type=F32),
                jnp.dot(k_sub, qt_b[:, n0:], preferred_element_type=F32))

    first_diag = i * sub_per_q

    def body(j, carry):
        m_a, m_b, acc_a, acc_b, s_a, s_b = carry
        for c in range(sub_per_q):
            sub = j * sub_per_q + c
            s_a_next, s_b_next = scores(sub + 1, 0)
            m_a, acc_a = update(m_a, acc_a, s_a, vta_ref[sub])
            m_b, acc_b = update(m_b, acc_b, s_b, vtb_ref[sub])
            s_a, s_b = s_a_next, s_b_next
        return m_a, m_b, acc_a, acc_b, s_a, s_b

    carry = (jnp.full((1, tq), -jnp.inf, F32), jnp.full((1, tq), -jnp.inf, F32),
             jnp.zeros((LANES, tq), F32), jnp.zeros((LANES, tq), F32)) + scores(0, 0)
    m_a, m_b, acc_a, acc_b, s_a, s_b = lax.fori_loop(0, i, body, carry)

    keep = lax.broadcasted_iota(jnp.int32, (tk, tk), 0) <= lax.broadcasted_iota(jnp.int32, (tk, tk), 1)
    for c in range(sub_per_q):
        n0 = c * tk
        if c + 1 < sub_per_q:
            s_a_next, s_b_next = scores(first_diag + c + 1, n0 + tk)

        def masked(s):
            diag = jnp.where(keep, s[:, :tk], -jnp.inf)
            return diag if n0 + tk == tq else jnp.concatenate([diag, s[:, tk:]], axis=1)

        def rejoin(old, new):
            return new if n0 == 0 else jnp.concatenate([old[:, :n0], new], axis=1)

        mn_a, an_a = update(m_a[:, n0:], acc_a[:, n0:], masked(s_a), vta_ref[first_diag + c])
        mn_b, an_b = update(m_b[:, n0:], acc_b[:, n0:], masked(s_b), vtb_ref[first_diag + c])
        m_a, m_b = rejoin(m_a, mn_a), rejoin(m_b, mn_b)
        acc_a, acc_b = rejoin(acc_a, an_a), rejoin(acc_b, an_b)
        if c + 1 < sub_per_q:
            s_a, s_b = s_a_next, s_b_next
    o_t = jnp.concatenate([acc_a[:HEAD_DIM] / acc_a[HEAD_DIM:HEAD_DIM + 1],
                           acc_b[HEAD_DIM:] / acc_b[0:1]], axis=0)
    o_ref[...] = (o_t.T * _silu(z_ref[...].astype(F32))).astype(BF16)


def _attention(proj, fcum, bsz, seqlen, d_model, n_heads, tq):
    n_hp = n_heads // 2
    per_row = d_model // LANES
    nq = seqlen // tq
    return pl.pallas_call(
        functools.partial(_attn_kernel, n_heads=n_heads, tq=tq),
        grid=(bsz, n_hp, nq),
        in_specs=[
            pl.BlockSpec((tq, LANES), lambda b, h, i: (b * nq + i, COL_Q * per_row + h)),
            pl.BlockSpec((seqlen, LANES), lambda b, h, i: (b, COL_K * per_row + h)),
            pl.BlockSpec((seqlen, LANES), lambda b, h, i: (b, COL_V * per_row + h)),
            pl.BlockSpec((seqlen, LANES), lambda b, h, i: (b, 0)),
            pl.BlockSpec((tq, LANES), lambda b, h, i: (b * nq + i, COL_Z_ATT * per_row + h)),
        ],
        out_specs=pl.BlockSpec((tq, LANES), lambda b, h, i: (b * nq + i, h)),
        out_shape=jax.ShapeDtypeStruct((bsz * seqlen, d_model), BF16),
        scratch_shapes=[
            pltpu.VMEM((seqlen, 2 * LANES), BF16),
            pltpu.VMEM((seqlen // KV_SUB, LANES, KV_SUB), BF16),
            pltpu.VMEM((seqlen // KV_SUB, LANES, KV_SUB), BF16),
        ],
        compiler_params=pltpu.CompilerParams(
            dimension_semantics=("arbitrary", "arbitrary", "arbitrary"), vmem_limit_bytes=VMEM_LIMIT),
        name="fox_attention",
    )(proj, proj, proj, fcum, proj)


def _ssd_kernel(xs_ref, bc_ref, z_ref, s_ref, cw_ref, cb_ref, dtb_ref, alog_ref, dskip_ref, ng_ref,
                expand_ref, y_ref, buf_ref, st_ref, *, n_heads):
    c = pl.program_id(1)
    q = SSM_CHUNK
    d_inner = xs_ref.shape[1]
    gn = SSM_GROUPS * SSM_STATE
    heads_per_group = n_heads // SSM_GROUPS
    gw = heads_per_group * HEAD_DIM
    hdr = 8

    @pl.when(c == 0)
    def _():
        buf_ref[:hdr, :] = jnp.zeros((hdr, buf_ref.shape[1]), F32)
        st_ref[...] = jnp.zeros(st_ref.shape, F32)

    buf_ref[hdr:, :d_inner] = xs_ref[...].astype(F32)
    buf_ref[hdr:, d_inner:] = bc_ref[...].astype(F32)
    conv = cb_ref[...] + cw_ref[SSM_CONV - 1:SSM_CONV, :] * buf_ref[hdr:, :]
    for jj in range(SSM_CONV - 1):
        off = hdr - (SSM_CONV - 1) + jj
        conv = conv + cw_ref[jj:jj + 1, :] * buf_ref[off:off + q, :]
    tail = buf_ref[q:, :]
    buf_ref[:hdr, :] = tail
    act = _silu(conv)
    xs = act[:, :d_inner]
    b16 = act[:, d_inner:d_inner + gn].astype(BF16)
    c16 = act[:, d_inner + gn:].astype(BF16)

    lane = lax.broadcasted_iota(jnp.int32, (q, LANES), 1)
    head_lane = lane < n_heads
    dt = jnp.where(head_lane, _softplus(s_ref[...] + dtb_ref[...]), 0.0)
    da = dt * (-jnp.exp(alog_ref[...]))
    tri = _lower_tri(q)
    cs = _tri_cumsum(tri.astype(BF16), da)
    cs_t = cs.T
    dt_t = dt.T
    last = cs[q - 1:q, :]
    w_end = dt * jnp.exp(last - cs)
    ecs = jnp.exp(cs)

    def expand(v):
        hi, mid, _ = _split3(v)
        e = expand_ref[...]
        return jnp.dot(hi, e, preferred_element_type=F32) + jnp.dot(mid, e, preferred_element_type=F32)

    w_e = expand(w_end)
    ecs_e = expand(ecs)
    xw16 = (xs * w_e).astype(BF16)

    first_half = (lax.broadcasted_iota(jnp.int32, (q, d_inner), 1) % LANES) < HEAD_DIM
    xs_first16 = jnp.where(first_half, xs, 0.0).astype(BF16)
    xs_second16 = jnp.where(first_half, 0.0, xs).astype(BF16)

    y_parts = []
    for g in range(SSM_GROUPS):
        bg = b16[:, g * SSM_STATE:(g + 1) * SSM_STATE]
        cg = c16[:, g * SSM_STATE:(g + 1) * SSM_STATE]
        cb = lax.dot_general(cg, bg, (((1,), (1,)), ((), ())), preferred_element_type=F32)
        st_prev = st_ref[g]
        y_off = jnp.dot(cg, st_prev.astype(BF16), preferred_element_type=F32)
        for pair in range(heads_per_group // 2):
            lo_lane = g * gw + pair * LANES
            y_pair = None
            for half in range(2):
                h = g * heads_per_group + pair * 2 + half
                seg = cs[:, h:h + 1] - cs_t[h:h + 1, :]
                decay = jnp.exp(jnp.where(tri, seg, -jnp.inf))
                mmat = (cb * decay * dt_t[h:h + 1, :]).astype(BF16)
                x_half = (xs_first16 if half == 0 else xs_second16)[:, lo_lane:lo_lane + LANES]
                contrib = jnp.dot(mmat, x_half, preferred_element_type=F32)
                y_pair = contrib if y_pair is None else y_pair + contrib
            y_pair = y_pair + y_off[:, pair * LANES:(pair + 1) * LANES] * ecs_e[:, lo_lane:lo_lane + LANES]
            y_parts.append(y_pair)
        bg_t = bg.astype(F32).T.astype(BF16)
        new_st = jnp.dot(bg_t, xw16[:, g * gw:(g + 1) * gw], preferred_element_type=F32)
        st_ref[g] = st_prev * ecs_e[q - 1:q, g * gw:(g + 1) * gw] + new_st

    y = jnp.concatenate(y_parts, axis=1) + xs * dskip_ref[...]
    yg = y * _silu(z_ref[...].astype(F32))
    outs = []
    for g in range(SSM_GROUPS):
        blk = yg[:, g * gw:(g + 1) * gw]
        ms = jnp.mean(blk * blk, axis=-1, keepdims=True)
        outs.append(blk * lax.rsqrt(ms + RMS_EPS))
    y_ref[...] = (jnp.concatenate(outs, axis=1) * ng_ref[...]).astype(BF16)


def _ssd(proj, small, conv_w, conv_b, dtb_vec, alog_vec, dskip_e, norm_g, expand_m, bsz, seqlen, d_model, n_heads):
    nc = seqlen // SSM_CHUNK
    q = SSM_CHUNK
    conv_dim = conv_w.shape[1]
    gw = (n_heads // SSM_GROUPS) * HEAD_DIM
    return pl.pallas_call(
        functools.partial(_ssd_kernel, n_heads=n_heads),
        grid=(bsz, nc),
        in_specs=[
            pl.BlockSpec((q, d_model), lambda b, c: (b * nc + c, COL_XS)),
            pl.BlockSpec((q, d_model), lambda b, c: (b * nc + c, COL_BC)),
            pl.BlockSpec((q, d_model), lambda b, c: (b * nc + c, COL_Z_SSM)),
            pl.BlockSpec((q, LANES), lambda b, c: (b * nc + c, 0)),
            pl.BlockSpec((SSM_CONV, conv_dim), lambda b, c: (0, 0)),
            pl.BlockSpec((1, conv_dim), lambda b, c: (0, 0)),
            pl.BlockSpec((1, LANES), lambda b, c: (0, 0)),
            pl.BlockSpec((1, LANES), lambda b, c: (0, 0)),
            pl.BlockSpec((1, d_model), lambda b, c: (0, 0)),
            pl.BlockSpec((1, d_model), lambda b, c: (0, 0)),
            pl.BlockSpec((LANES, d_model), lambda b, c: (0, 0)),
        ],
        out_specs=pl.BlockSpec((q, d_model), lambda b, c: (b * nc + c, 0)),
        out_shape=jax.ShapeDtypeStruct((bsz * seqlen, d_model), BF16),
        scratch_shapes=[
            pltpu.VMEM((q + 8, conv_dim), F32),
            pltpu.VMEM((SSM_GROUPS, SSM_STATE, gw), F32),
        ],
        compiler_params=pltpu.CompilerParams(
            dimension_semantics=("arbitrary", "arbitrary"), vmem_limit_bytes=VMEM_LIMIT),
        name="ssd_branch",
    )(proj, proj, proj, small, conv_w, conv_b, dtb_vec, alog_vec, dskip_e, norm_g, expand_m)


def _merge_kernel(ys_ref, ya_ref, gs_ref, ga_ref, x_ref, ws_ref, wa_ref, wo_ref, pg_ref, o_ref):
    p_ssm = jnp.dot(ys_ref[...], ws_ref[...], preferred_element_type=F32)
    p_att = jnp.dot(ya_ref[...], wa_ref[...], preferred_element_type=F32)
    merged = _sigmoid(gs_ref[...].astype(F32)) * p_ssm + _sigmoid(ga_ref[...].astype(F32)) * p_att
    out = jnp.dot(merged.astype(BF16), wo_ref[...], preferred_element_type=F32)
    ms = jnp.mean(out * out, axis=-1, keepdims=True)
    o_ref[...] = x_ref[...] + out * lax.rsqrt(ms + RMS_EPS) * pg_ref[...]


def _merge(y_ssm, y_att, proj, x2, w_ssm, w_att, w_out, post_g, tm):
    m, d = x2.shape
    row = lambda i: (i, 0)
    full = lambda i: (0, 0)
    return pl.pallas_call(
        _merge_kernel,
        grid=(m // tm,),
        in_specs=[
            pl.BlockSpec((tm, d), row),
            pl.BlockSpec((tm, d), row),
            pl.BlockSpec((tm, d), lambda i: (i, COL_G_SSM)),
            pl.BlockSpec((tm, d), lambda i: (i, COL_G_ATT)),
            pl.BlockSpec((tm, d), row),
            pl.BlockSpec((d, d), full),
            pl.BlockSpec((d, d), full),
            pl.BlockSpec((d, d), full),
            pl.BlockSpec((1, d), full),
        ],
        out_specs=pl.BlockSpec((tm, d), row),
        out_shape=jax.ShapeDtypeStruct((m, d), F32),
        compiler_params=pltpu.CompilerParams(
            dimension_semantics=("arbitrary",), vmem_limit_bytes=VMEM_LIMIT),
        name="gated_merge",
    )(y_ssm, y_att, proj, proj, x2, w_ssm, w_att, w_out, post_g)


def _pad_lanes(v, offset):
    out = jnp.zeros((1, LANES), F32)
    return lax.dynamic_update_slice(out, v.reshape(1, -1).astype(F32), (0, offset))


def _layer(x, pre_g, w_in, conv_w, conv_b, dt_bias, a_log, d_skip, ssm_norm_g, fgate_b,
           w_branch_ssm, w_branch_att, w_out, post_g):
    bsz, seqlen, d = x.shape
    n_heads = d // HEAD_DIM
    gn = SSM_GROUPS * SSM_STATE
    sizes = (d, d + 2 * gn, n_heads, d, d, d, n_heads, d, d, d)
    offs = [0]
    for s in sizes:
        offs.append(offs[-1] + s)
    z_ssm, xbc, dt_w, q_w, k_w, v_w, f_w, z_att, g_ssm, g_att = (w_in[:, offs[n]:offs[n + 1]] for n in range(10))
    scale = HEAD_DIM ** -0.5
    w_big = jnp.concatenate([z_ssm, xbc, q_w * scale, k_w, v_w, z_att, g_ssm, g_att], axis=1).astype(BF16)
    w_small = jnp.concatenate([dt_w, f_w, jnp.zeros((d, LANES - 2 * n_heads), w_in.dtype)], axis=1).astype(BF16)

    x2 = x.reshape(bsz * seqlen, d)
    proj, small = _in_proj(x2, pre_g.reshape(1, d), w_big, w_small, tm=min(1024, bsz * seqlen))

    fcum = _fcumsum(small, _pad_lanes(fgate_b, n_heads), bsz, seqlen, n_heads)
    y_att = _attention(proj, fcum, bsz, seqlen, d, n_heads, tq=min(1024, seqlen))

    head_of_col = jnp.arange(d) // HEAD_DIM
    expand_m = (jnp.arange(LANES)[:, None] == head_of_col[None, :]).astype(BF16)
    y_ssm = _ssd(proj, small, conv_w, conv_b.reshape(1, -1), _pad_lanes(dt_bias, 0), _pad_lanes(a_log, 0),
                 jnp.repeat(d_skip.astype(F32), HEAD_DIM).reshape(1, d), ssm_norm_g.reshape(1, d), expand_m,
                 bsz, seqlen, d, n_heads)

    out = _merge(y_ssm, y_att, proj, x2, w_branch_ssm.astype(BF16), w_branch_att.astype(BF16),
                 w_out.astype(BF16), post_g.reshape(1, d), tm=min(512, bsz * seqlen))
    return out.reshape(bsz, seqlen, d)


def kernel(x, pre_norm_g, w_in, conv_w, conv_b, dt_bias, a_log, d_skip, ssm_norm_g, fgate_b,
           w_branch_ssm, w_branch_att, w_out, post_norm_g):
    for i in range(pre_norm_g.shape[0]):
        x = _layer(x, pre_norm_g[i], w_in[i], conv_w[i], conv_b[i], dt_bias[i], a_log[i], d_skip[i],
                   ssm_norm_g[i], fgate_b[i], w_branch_ssm[i], w_branch_att[i], w_out[i], post_norm_g[i])
    return x
```

```python
import functools

import jax
import jax.numpy as jnp
from jax import lax
from jax.experimental import pallas as pl
from jax.experimental.pallas import tpu as pltpu

F32 = jnp.float32
BF16 = jnp.bfloat16

RMS_EPS = 1e-6
HEAD_DIM = 64
SSM_GROUPS = 4
SSM_STATE = 128
SSM_CONV = 4
SSM_CHUNK = 128
KV_SUB = 256
LANES = 128
VMEM_LIMIT = 48 * 1024 * 1024

COL_Z_SSM, COL_XS, COL_BC, COL_K, COL_Z_ATT, COL_G_SSM, COL_G_ATT = range(7)
N_COL_BLOCKS = 7
N_T_BLOCKS = 2
LOG2E = 1.4426950408889634
F_LANE = 0
DT_LANE = 4


def _sigmoid(x):
    return 1.0 / (1.0 + jnp.exp(-x))


def _silu(x):
    return x * _sigmoid(x)


def _softplus(x):
    return jnp.maximum(x, 0.0) + jnp.log(1.0 + jnp.exp(-jnp.abs(x)))


def _split3(x):
    hi = x.astype(BF16)
    r1 = x - hi.astype(F32)
    mid = r1.astype(BF16)
    lo = (r1 - mid.astype(F32)).astype(BF16)
    return hi, mid, lo


def _tri_cumsum(tri_bf16, x):
    hi, mid, lo = _split3(x)
    dot = functools.partial(jnp.dot, preferred_element_type=F32)
    return dot(tri_bf16, hi) + dot(tri_bf16, mid) + dot(tri_bf16, lo)


def _lower_tri(n):
    r = lax.broadcasted_iota(jnp.int32, (n, n), 0)
    c = lax.broadcasted_iota(jnp.int32, (n, n), 1)
    return r >= c


def _in_proj_kernel(x_ref, g_ref, w_ref, wt_ref, ws_ref, p_ref, s_ref, qt_ref, vta_ref, vtb_ref, h_ref):
    j = pl.program_id(1)
    nt = (((1,), (1,)), ((), ()))

    @pl.when(j == 0)
    def _():
        x = x_ref[...]
        ms = jnp.mean(x * x, axis=-1, keepdims=True)
        h = (x * lax.rsqrt(ms + RMS_EPS) * g_ref[...]).astype(BF16)
        h_ref[...] = h
        s_ref[...] = jnp.dot(h, ws_ref[...], preferred_element_type=F32)

    @pl.when(j < N_COL_BLOCKS)
    def _():
        p_ref[...] = jnp.dot(h_ref[...], w_ref[...], preferred_element_type=F32).astype(BF16)

    @pl.when(j == N_COL_BLOCKS)
    def _():
        qt_ref[...] = lax.dot_general(wt_ref[...], h_ref[...], nt, preferred_element_type=F32).astype(BF16)

    @pl.when(j == N_COL_BLOCKS + 1)
    def _():
        vt = lax.dot_general(wt_ref[...], h_ref[...], nt, preferred_element_type=F32)
        first = (lax.broadcasted_iota(jnp.int32, vt.shape, 0) % LANES) < HEAD_DIM
        vta = jnp.where(first, vt, 1.0).astype(BF16)
        vtb = jnp.where(first, 1.0, vt).astype(BF16)
        for c in range(vta_ref.shape[0]):
            vta_ref[c] = vta[:, c * KV_SUB:(c + 1) * KV_SUB]
            vtb_ref[c] = vtb[:, c * KV_SUB:(c + 1) * KV_SUB]


def _in_proj(x2, g, w_big, w_t, w_small, tm):
    m, d = x2.shape
    last = N_COL_BLOCKS - 1
    sub = tm // KV_SUB
    vt_shape = jax.ShapeDtypeStruct((m // KV_SUB, d, KV_SUB), BF16)
    return pl.pallas_call(
        _in_proj_kernel,
        grid=(m // tm, N_COL_BLOCKS + N_T_BLOCKS),
        in_specs=[
            pl.BlockSpec((tm, d), lambda i, j: (i, 0)),
            pl.BlockSpec((1, d), lambda i, j: (0, 0)),
            pl.BlockSpec((d, d), lambda i, j: (0, jnp.minimum(j, last))),
            pl.BlockSpec((d, d), lambda i, j: (jnp.maximum(j - N_COL_BLOCKS, 0), 0)),
            pl.BlockSpec((d, LANES), lambda i, j: (0, 0)),
        ],
        out_specs=[
            pl.BlockSpec((tm, d), lambda i, j: (i, jnp.minimum(j, last))),
            pl.BlockSpec((tm, LANES), lambda i, j: (i, 0)),
            pl.BlockSpec((d, tm), lambda i, j: (0, i)),
            pl.BlockSpec((sub, d, KV_SUB), lambda i, j: (i, 0, 0)),
            pl.BlockSpec((sub, d, KV_SUB), lambda i, j: (i, 0, 0)),
        ],
        out_shape=[
            jax.ShapeDtypeStruct((m, N_COL_BLOCKS * d), BF16),
            jax.ShapeDtypeStruct((m, LANES), F32),
            jax.ShapeDtypeStruct((d, m), BF16),
            vt_shape,
            vt_shape,
        ],
        scratch_shapes=[pltpu.VMEM((tm, d), BF16)],
        compiler_params=pltpu.CompilerParams(
            dimension_semantics=("arbitrary", "arbitrary"), vmem_limit_bytes=VMEM_LIMIT),
        name="in_proj",
    )(x2, g, w_big, w_t, w_small)


def _fcumsum_kernel(s_ref, b_ref, gk_ref, gqt_ref):
    seqlen = s_ref.shape[0]
    blk = SSM_CHUNK
    tri = _lower_tri(blk).astype(BF16)
    slot = lax.broadcasted_iota(jnp.int32, (blk, LANES), 1) % 8
    carry = jnp.zeros((1, LANES), F32)
    for c in range(seqlen // blk):
        rows = slice(c * blk, (c + 1) * blk)
        raw = s_ref[rows, :] + b_ref[...]
        logf = jnp.where(slot == F_LANE, -_softplus(-raw), 0.0)
        f_blk = _tri_cumsum(tri, logf) + carry
        carry = f_blk[blk - 1:blk, :]
        hi, mid, lo = (p.astype(F32) for p in _split3(f_blk * LOG2E))
        gq = hi + pltpu.roll(mid, 1, axis=1) + pltpu.roll(lo, 2, axis=1) + jnp.where((slot >= 3) & (slot < 6), 1.0, 0.0)
        gk = jnp.where(slot < 3, 1.0, 0.0) - (pltpu.roll(hi, 3, axis=1) + pltpu.roll(mid, 4, axis=1)
                                              + pltpu.roll(lo, 5, axis=1))
        gk_ref[rows, :] = gk.astype(BF16)
        gqt_ref[:, rows] = gq.T.astype(BF16)


def _fcumsum(small, fb_vec, bsz, seqlen):
    return pl.pallas_call(
        _fcumsum_kernel,
        grid=(bsz,),
        in_specs=[
            pl.BlockSpec((seqlen, LANES), lambda b: (b, 0)),
            pl.BlockSpec((1, LANES), lambda b: (0, 0)),
        ],
        out_specs=[
            pl.BlockSpec((seqlen, LANES), lambda b: (b, 0)),
            pl.BlockSpec((LANES, seqlen), lambda b: (b, 0)),
        ],
        out_shape=[
            jax.ShapeDtypeStruct((bsz * seqlen, LANES), BF16),
            jax.ShapeDtypeStruct((bsz * LANES, seqlen), BF16),
        ],
        compiler_params=pltpu.CompilerParams(
            dimension_semantics=("arbitrary",), vmem_limit_bytes=VMEM_LIMIT),
        name="fcumsum",
    )(small, fb_vec)


def _attn_kernel(qt_ref, k_ref, gk_ref, vta_ref, vtb_ref, gqt_ref, o_ref,
                 qaug_ref, s_a0, s_a1, s_b0, s_b1, acc_ref, m_ref, *, tq):
    hp = pl.program_id(1)
    i = pl.program_id(2)
    tk = KV_SUB
    sub_per_q = tq // tk
    s_refs = ((s_a0, s_a1), (s_b0, s_b1))
    vt_refs = (vta_ref, vtb_ref)

    grp = lax.broadcasted_iota(jnp.int32, (LANES, tq), 0) // 8
    gq_t = gqt_ref[...].astype(F32)
    zeros_half = jnp.zeros((HEAD_DIM, tq), BF16)
    qaug_ref[0, :HEAD_DIM] = qt_ref[:HEAD_DIM, :]
    qaug_ref[0, HEAD_DIM:LANES] = zeros_half
    qaug_ref[1, :HEAD_DIM] = zeros_half
    qaug_ref[1, HEAD_DIM:LANES] = qt_ref[HEAD_DIM:, :]
    for h in range(2):
        qaug_ref[h, LANES:] = jnp.where(grp == 2 * hp + h, gq_t, 0.0).astype(BF16)
    acc_ref[...] = jnp.zeros(acc_ref.shape, F32)
    m_ref[...] = jnp.full(m_ref.shape, -jnp.inf, F32)

    def qk(sub, n0, slot):
        rows = pl.ds(pl.multiple_of(sub * tk, tk), tk)
        k_sub = jnp.concatenate([k_ref[rows, :], gk_ref[rows, :]], axis=1)
        for h in range(2):
            s_refs[h][slot][:, n0:] = jnp.dot(k_sub, qaug_ref[h, :, n0:], preferred_element_type=F32)

    def update(sub, n0, slot, keep=None):
        for h in range(2):
            s = s_refs[h][slot][:, n0:]
            if keep is not None:
                diag = jnp.where(keep, s[:, :tk], -jnp.inf)
                s = diag if n0 + tk == tq else jnp.concatenate([diag, s[:, tk:]], axis=1)
            m = m_ref[h, 0:1, n0:]
            mn = jnp.maximum(m, jnp.max(s, axis=0, keepdims=True))
            p = jnp.exp2(s - mn).astype(BF16)
            acc_ref[h, :, n0:] = (acc_ref[h, :, n0:] * jnp.exp2(m - mn)
                                  + jnp.dot(vt_refs[h][sub], p, preferred_element_type=F32))
            m_ref[h, 0:1, n0:] = mn

    first_diag = i * sub_per_q
    qk(0, 0, 0)

    def body(j, carry):
        for c in range(sub_per_q):
            sub = j * sub_per_q + c
            qk(sub + 1, 0, (c + 1) % 2)
            update(sub, 0, c % 2)
        return carry

    lax.fori_loop(0, i, body, 0)

    keep = lax.broadcasted_iota(jnp.int32, (tk, tk), 0) <= lax.broadcasted_iota(jnp.int32, (tk, tk), 1)
    for c in range(sub_per_q):
        n0 = c * tk
        if c + 1 < sub_per_q:
            qk(first_diag + c + 1, n0 + tk, (c + 1) % 2)
        update(first_diag + c, n0, c % 2, keep)

    inv_a = 1.0 / acc_ref[0, HEAD_DIM:HEAD_DIM + 1, :]
    inv_b = 1.0 / acc_ref[1, 0:1, :]
    o_t = jnp.concatenate([acc_ref[0, :HEAD_DIM, :] * inv_a, acc_ref[1, HEAD_DIM:, :] * inv_b], axis=0)
    o_ref[...] = o_t.T.astype(BF16)


def _attention(proj, q_t, vta, vtb, gk, gq_t, bsz, seqlen, d_model, tq):
    assert (tq // KV_SUB) % 2 == 0, "score buffers alternate per key sub-tile"
    n_hp = d_model // LANES
    per_row = d_model // LANES
    nq = seqlen // tq
    n_sub = seqlen // KV_SUB
    score_buf = pltpu.VMEM((KV_SUB, tq), F32)
    return pl.pallas_call(
        functools.partial(_attn_kernel, tq=tq),
        grid=(bsz, n_hp, nq),
        in_specs=[
            pl.BlockSpec((LANES, tq), lambda b, h, i: (h, b * nq + i)),
            pl.BlockSpec((seqlen, LANES), lambda b, h, i: (b, COL_K * per_row + h)),
            pl.BlockSpec((seqlen, LANES), lambda b, h, i: (b, 0)),
            pl.BlockSpec((n_sub, LANES, KV_SUB), lambda b, h, i: (b, h, 0)),
            pl.BlockSpec((n_sub, LANES, KV_SUB), lambda b, h, i: (b, h, 0)),
            pl.BlockSpec((LANES, tq), lambda b, h, i: (b, i)),
        ],
        out_specs=pl.BlockSpec((tq, LANES), lambda b, h, i: (b * nq + i, h)),
        out_shape=jax.ShapeDtypeStruct((bsz * seqlen, d_model), BF16),
        scratch_shapes=[
            pltpu.VMEM((2, 2 * LANES, tq), BF16),
            score_buf, score_buf, score_buf, score_buf,
            pltpu.VMEM((2, LANES, tq), F32),
            pltpu.VMEM((2, 8, tq), F32),
        ],
        compiler_params=pltpu.CompilerParams(
            dimension_semantics=("arbitrary", "arbitrary", "arbitrary"), vmem_limit_bytes=VMEM_LIMIT),
        name="fox_attention",
    )(q_t, proj, gk, vta, vtb, gq_t)


def _ssd_kernel(xs_ref, bc_ref, z_ref, s_ref, cw_ref, cb_ref, dtb_ref, alog_ref, dskip_ref, ng_ref,
                expand_ref, y_ref, buf_ref, st_ref, *, n_heads):
    c = pl.program_id(1)
    q = SSM_CHUNK
    d_inner = xs_ref.shape[1]
    gn = SSM_GROUPS * SSM_STATE
    heads_per_group = n_heads // SSM_GROUPS
    gw = heads_per_group * HEAD_DIM
    hdr = 8

    @pl.when(c == 0)
    def _():
        buf_ref[:hdr, :] = jnp.zeros((hdr, buf_ref.shape[1]), F32)
        st_ref[...] = jnp.zeros(st_ref.shape, F32)

    buf_ref[hdr:, :d_inner] = xs_ref[...].astype(F32)
    buf_ref[hdr:, d_inner:] = bc_ref[...].astype(F32)
    conv = cb_ref[...] + cw_ref[SSM_CONV - 1:SSM_CONV, :] * buf_ref[hdr:, :]
    for jj in range(SSM_CONV - 1):
        off = hdr - (SSM_CONV - 1) + jj
        conv = conv + cw_ref[jj:jj + 1, :] * buf_ref[off:off + q, :]
    tail = buf_ref[q:, :]
    buf_ref[:hdr, :] = tail
    act = _silu(conv)
    xs = act[:, :d_inner]
    b16 = act[:, d_inner:d_inner + gn].astype(BF16)
    c16 = act[:, d_inner + gn:].astype(BF16)

    lane = lax.broadcasted_iota(jnp.int32, (q, LANES), 1)
    head_lane = lane % 8 == DT_LANE
    dt = jnp.where(head_lane, _softplus(s_ref[...] + dtb_ref[...]), 0.0)
    da = dt * (-jnp.exp(alog_ref[...]))
    tri = _lower_tri(q)
    cs = _tri_cumsum(tri.astype(BF16), da)
    cs_t = cs.T
    dt_t = dt.T
    last = cs[q - 1:q, :]
    w_end = dt * jnp.exp(last - cs)
    ecs = jnp.exp(cs)

    def expand(v):
        hi, mid, _ = _split3(v)
        e = expand_ref[...]
        return jnp.dot(hi, e, preferred_element_type=F32) + jnp.dot(mid, e, preferred_element_type=F32)

    w_e = expand(w_end)
    ecs_e = expand(ecs)
    xw16 = (xs * w_e).astype(BF16)

    first_half = (lax.broadcasted_iota(jnp.int32, (q, d_inner), 1) % LANES) < HEAD_DIM
    xs_first16 = jnp.where(first_half, xs, 0.0).astype(BF16)
    xs_second16 = jnp.where(first_half, 0.0, xs).astype(BF16)

    y_parts = []
    for g in range(SSM_GROUPS):
        bg = b16[:, g * SSM_STATE:(g + 1) * SSM_STATE]
        cg = c16[:, g * SSM_STATE:(g + 1) * SSM_STATE]
        cb = lax.dot_general(cg, bg, (((1,), (1,)), ((), ())), preferred_element_type=F32)
        st_prev = st_ref[g]
        y_off = jnp.dot(cg, st_prev.astype(BF16), preferred_element_type=F32)
        for pair in range(heads_per_group // 2):
            lo_lane = g * gw + pair * LANES
            y_pair = None
            for half in range(2):
                h = 8 * (g * heads_per_group + pair * 2 + half) + DT_LANE
                seg = cs[:, h:h + 1] - cs_t[h:h + 1, :]
                decay = jnp.exp(jnp.where(tri, seg, -jnp.inf))
                mmat = (cb * decay * dt_t[h:h + 1, :]).astype(BF16)
                x_half = (xs_first16 if half == 0 else xs_second16)[:, lo_lane:lo_lane + LANES]
                contrib = jnp.dot(mmat, x_half, preferred_element_type=F32)
                y_pair = contrib if y_pair is None else y_pair + contrib
            y_pair = y_pair + y_off[:, pair * LANES:(pair + 1) * LANES] * ecs_e[:, lo_lane:lo_lane + LANES]
            y_parts.append(y_pair)
        bg_t = bg.astype(F32).T.astype(BF16)
        new_st = jnp.dot(bg_t, xw16[:, g * gw:(g + 1) * gw], preferred_element_type=F32)
        st_ref[g] = st_prev * ecs_e[q - 1:q, g * gw:(g + 1) * gw] + new_st

    y = jnp.concatenate(y_parts, axis=1) + xs * dskip_ref[...]
    yg = y * _silu(z_ref[...].astype(F32))
    outs = []
    for g in range(SSM_GROUPS):
        blk = yg[:, g * gw:(g + 1) * gw]
        ms = jnp.mean(blk * blk, axis=-1, keepdims=True)
        outs.append(blk * lax.rsqrt(ms + RMS_EPS))
    y_ref[...] = (jnp.concatenate(outs, axis=1) * ng_ref[...]).astype(BF16)


def _ssd(proj, small, conv_w, conv_b, dtb_vec, alog_vec, dskip_e, norm_g, expand_m, bsz, seqlen, d_model, n_heads):
    nc = seqlen // SSM_CHUNK
    q = SSM_CHUNK
    conv_dim = conv_w.shape[1]
    gw = (n_heads // SSM_GROUPS) * HEAD_DIM
    return pl.pallas_call(
        functools.partial(_ssd_kernel, n_heads=n_heads),
        grid=(bsz, nc),
        in_specs=[
            pl.BlockSpec((q, d_model), lambda b, c: (b * nc + c, COL_XS)),
            pl.BlockSpec((q, d_model), lambda b, c: (b * nc + c, COL_BC)),
            pl.BlockSpec((q, d_model), lambda b, c: (b * nc + c, COL_Z_SSM)),
            pl.BlockSpec((q, LANES), lambda b, c: (b * nc + c, 0)),
            pl.BlockSpec((SSM_CONV, conv_dim), lambda b, c: (0, 0)),
            pl.BlockSpec((1, conv_dim), lambda b, c: (0, 0)),
            pl.BlockSpec((1, LANES), lambda b, c: (0, 0)),
            pl.BlockSpec((1, LANES), lambda b, c: (0, 0)),
            pl.BlockSpec((1, d_model), lambda b, c: (0, 0)),
            pl.BlockSpec((1, d_model), lambda b, c: (0, 0)),
            pl.BlockSpec((LANES, d_model), lambda b, c: (0, 0)),
        ],
        out_specs=pl.BlockSpec((q, d_model), lambda b, c: (b * nc + c, 0)),
        out_shape=jax.ShapeDtypeStruct((bsz * seqlen, d_model), BF16),
        scratch_shapes=[
            pltpu.VMEM((q + 8, conv_dim), F32),
            pltpu.VMEM((SSM_GROUPS, SSM_STATE, gw), F32),
        ],
        compiler_params=pltpu.CompilerParams(
            dimension_semantics=("arbitrary", "arbitrary"), vmem_limit_bytes=VMEM_LIMIT),
        name="ssd_branch",
    )(proj, proj, proj, small, conv_w, conv_b, dtb_vec, alog_vec, dskip_e, norm_g, expand_m)


def _merge_kernel(ys_ref, oa_ref, za_ref, gs_ref, ga_ref, x_ref, ws_ref, wa_ref, wo_ref, pg_ref, o_ref):
    y_att = (oa_ref[...].astype(F32) * _silu(za_ref[...].astype(F32))).astype(BF16)
    p_ssm = jnp.dot(ys_ref[...], ws_ref[...], preferred_element_type=F32)
    p_att = jnp.dot(y_att, wa_ref[...], preferred_element_type=F32)
    merged = _sigmoid(gs_ref[...].astype(F32)) * p_ssm + _sigmoid(ga_ref[...].astype(F32)) * p_att
    out = jnp.dot(merged.astype(BF16), wo_ref[...], preferred_element_type=F32)
    ms = jnp.mean(out * out, axis=-1, keepdims=True)
    o_ref[...] = x_ref[...] + out * lax.rsqrt(ms + RMS_EPS) * pg_ref[...]


def _merge(y_ssm, o_att, proj, x2, w_ssm, w_att, w_out, post_g, tm):
    m, d = x2.shape
    row = lambda i: (i, 0)
    full = lambda i: (0, 0)
    return pl.pallas_call(
        _merge_kernel,
        grid=(m // tm,),
        in_specs=[
            pl.BlockSpec((tm, d), row),
            pl.BlockSpec((tm, d), row),
            pl.BlockSpec((tm, d), lambda i: (i, COL_Z_ATT)),
            pl.BlockSpec((tm, d), lambda i: (i, COL_G_SSM)),
            pl.BlockSpec((tm, d), lambda i: (i, COL_G_ATT)),
            pl.BlockSpec((tm, d), row),
            pl.BlockSpec((d, d), full),
            pl.BlockSpec((d, d), full),
            pl.BlockSpec((d, d), full),
            pl.BlockSpec((1, d), full),
        ],
        out_specs=pl.BlockSpec((tm, d), row),
        out_shape=jax.ShapeDtypeStruct((m, d), F32),
        compiler_params=pltpu.CompilerParams(
            dimension_semantics=("arbitrary",), vmem_limit_bytes=VMEM_LIMIT),
        name="gated_merge",
    )(y_ssm, o_att, proj, proj, proj, x2, w_ssm, w_att, w_out, post_g)


def _head_lanes(v, slot):
    out = jnp.zeros((v.shape[0], 8), F32).at[:, slot].set(v.astype(F32))
    return out.reshape(1, LANES)


def _layer(x, pre_g, w_in, conv_w, conv_b, dt_bias, a_log, d_skip, ssm_norm_g, fgate_b,
           w_branch_ssm, w_branch_att, w_out, post_g):
    bsz, seqlen, d = x.shape
    n_heads = d // HEAD_DIM
    assert n_heads * 8 == LANES, "small-projection lane layout assumes 16 heads"
    gn = SSM_GROUPS * SSM_STATE
    sizes = (d, d + 2 * gn, n_heads, d, d, d, n_heads, d, d, d)
    offs = [0]
    for sz in sizes:
        offs.append(offs[-1] + sz)
    z_ssm, xbc, dt_w, q_w, k_w, v_w, f_w, z_att, g_ssm, g_att = (w_in[:, offs[n]:offs[n + 1]] for n in range(10))
    w_big = jnp.concatenate([z_ssm, xbc, k_w, z_att, g_ssm, g_att], axis=1).astype(BF16)
    w_t = jnp.concatenate([(q_w * (HEAD_DIM ** -0.5 * LOG2E)).T, v_w.T], axis=0).astype(BF16)
    w_small = jnp.zeros((d, n_heads, 8), w_in.dtype).at[:, :, F_LANE].set(f_w).at[:, :, DT_LANE].set(dt_w)
    w_small = w_small.reshape(d, LANES).astype(BF16)

    x2 = x.reshape(bsz * seqlen, d)
    proj, small, q_t, vta, vtb = _in_proj(x2, pre_g.reshape(1, d), w_big, w_t, w_small, tm=min(1024, bsz * seqlen))

    gk, gq_t = _fcumsum(small, _head_lanes(fgate_b, F_LANE), bsz, seqlen)
    o_att = _attention(proj, q_t, vta, vtb, gk, gq_t, bsz, seqlen, d, tq=min(2048, seqlen))

    head_of_col = jnp.arange(d) // HEAD_DIM
    expand_m = (jnp.arange(LANES)[:, None] == 8 * head_of_col[None, :] + DT_LANE).astype(BF16)
    y_ssm = _ssd(proj, small, conv_w, conv_b.reshape(1, -1), _head_lanes(dt_bias, DT_LANE), _head_lanes(a_log, DT_LANE),
                 jnp.repeat(d_skip.astype(F32), HEAD_DIM).reshape(1, d), ssm_norm_g.reshape(1, d), expand_m,
                 bsz, seqlen, d, n_heads)

    out = _merge(y_ssm, o_att, proj, x2, w_branch_ssm.astype(BF16), w_branch_att.astype(BF16),
                 w_out.astype(BF16), post_g.reshape(1, d), tm=min(512, bsz * seqlen))
    return out.reshape(bsz, seqlen, d)


def kernel(x, pre_norm_g, w_in, conv_w, conv_b, dt_bias, a_log, d_skip, ssm_norm_g, fgate_b,
           w_branch_ssm, w_branch_att, w_out, post_norm_g):
    for i in range(pre_norm_g.shape[0]):
        x = _layer(x, pre_norm_g[i], w_in[i], conv_w[i], conv_b[i], dt_bias[i], a_log[i], d_skip[i],
                   ssm_norm_g[i], fgate_b[i], w_branch_ssm[i], w_branch_att[i], w_out[i], post_norm_g[i])
    return x
```

```python
import functools

import jax
import jax.numpy as jnp
from jax import lax
from jax.experimental import pallas as pl
from jax.experimental.pallas import tpu as pltpu

F32 = jnp.float32
BF16 = jnp.bfloat16

RMS_EPS = 1e-6
HEAD_DIM = 64
SSM_GROUPS = 4
SSM_STATE = 128
SSM_CONV = 4
SSM_CHUNK = 128
CONV_COLS = 512
KV_SUB = 256
LANES = 128
VMEM_LIMIT = 48 * 1024 * 1024

COL_Z_SSM, COL_XS, COL_BC, COL_K, COL_Z_ATT, COL_G_SSM, COL_G_ATT = range(7)
N_COL_BLOCKS = 7
N_T_BLOCKS = 2
LOG2E = 1.4426950408889634
F_LANE = 0
DT_LANE = 4


def _sigmoid(x):
    return 0.5 * jnp.tanh(0.5 * x) + 0.5


def _silu(x):
    return x * _sigmoid(x)


def _softplus(x):
    return jnp.maximum(x, 0.0) + jnp.log(1.0 + jnp.exp(-jnp.abs(x)))


def _split3(x):
    hi = x.astype(BF16)
    r1 = x - hi.astype(F32)
    mid = r1.astype(BF16)
    lo = (r1 - mid.astype(F32)).astype(BF16)
    return hi, mid, lo


def _tri_cumsum(tri_bf16, x):
    hi, mid, lo = _split3(x)
    dot = functools.partial(jnp.dot, preferred_element_type=F32)
    return dot(tri_bf16, hi) + dot(tri_bf16, mid) + dot(tri_bf16, lo)


def _lower_tri(n):
    r = lax.broadcasted_iota(jnp.int32, (n, n), 0)
    c = lax.broadcasted_iota(jnp.int32, (n, n), 1)
    return r >= c


def _in_proj_kernel(x_ref, g_ref, w_ref, wt_ref, ws_ref, p_ref, s_ref, qt_ref, vta_ref, vtb_ref, h_ref):
    j = pl.program_id(1)
    nt = (((1,), (1,)), ((), ()))

    @pl.when(j == 0)
    def _():
        x = x_ref[...]
        ms = jnp.mean(x * x, axis=-1, keepdims=True)
        h = (x * lax.rsqrt(ms + RMS_EPS) * g_ref[...]).astype(BF16)
        h_ref[...] = h
        s_ref[...] = jnp.dot(h, ws_ref[...], preferred_element_type=F32)

    @pl.when(j < N_COL_BLOCKS)
    def _():
        p_ref[...] = jnp.dot(h_ref[...], w_ref[...], preferred_element_type=F32).astype(BF16)

    @pl.when(j == N_COL_BLOCKS)
    def _():
        qt_ref[...] = lax.dot_general(wt_ref[...], h_ref[...], nt, preferred_element_type=F32).astype(BF16)

    @pl.when(j == N_COL_BLOCKS + 1)
    def _():
        vt = lax.dot_general(wt_ref[...], h_ref[...], nt, preferred_element_type=F32)
        first = (lax.broadcasted_iota(jnp.int32, vt.shape, 0) % LANES) < HEAD_DIM
        vta = jnp.where(first, vt, 1.0).astype(BF16)
        vtb = jnp.where(first, 1.0, vt).astype(BF16)
        for c in range(vta_ref.shape[0]):
            vta_ref[c] = vta[:, c * KV_SUB:(c + 1) * KV_SUB]
            vtb_ref[c] = vtb[:, c * KV_SUB:(c + 1) * KV_SUB]


def _in_proj(x2, g, w_big, w_t, w_small, tm):
    m, d = x2.shape
    last = N_COL_BLOCKS - 1
    sub = tm // KV_SUB
    vt_shape = jax.ShapeDtypeStruct((m // KV_SUB, d, KV_SUB), BF16)
    return pl.pallas_call(
        _in_proj_kernel,
        grid=(m // tm, N_COL_BLOCKS + N_T_BLOCKS),
        in_specs=[
            pl.BlockSpec((tm, d), lambda i, j: (i, 0)),
            pl.BlockSpec((1, d), lambda i, j: (0, 0)),
            pl.BlockSpec((d, d), lambda i, j: (0, jnp.minimum(j, last))),
            pl.BlockSpec((d, d), lambda i, j: (jnp.maximum(j - N_COL_BLOCKS, 0), 0)),
            pl.BlockSpec((d, LANES), lambda i, j: (0, 0)),
        ],
        out_specs=[
            pl.BlockSpec((tm, d), lambda i, j: (i, jnp.minimum(j, last))),
            pl.BlockSpec((tm, LANES), lambda i, j: (i, 0)),
            pl.BlockSpec((d, tm), lambda i, j: (0, i)),
            pl.BlockSpec((sub, d, KV_SUB), lambda i, j: (i, 0, 0)),
            pl.BlockSpec((sub, d, KV_SUB), lambda i, j: (i, 0, 0)),
        ],
        out_shape=[
            jax.ShapeDtypeStruct((m, N_COL_BLOCKS * d), BF16),
            jax.ShapeDtypeStruct((m, LANES), F32),
            jax.ShapeDtypeStruct((d, m), BF16),
            vt_shape,
            vt_shape,
        ],
        scratch_shapes=[pltpu.VMEM((tm, d), BF16)],
        compiler_params=pltpu.CompilerParams(
            dimension_semantics=("arbitrary", "arbitrary"), vmem_limit_bytes=VMEM_LIMIT),
        name="in_proj",
    )(x2, g, w_big, w_t, w_small)


def _fcumsum_kernel(s_ref, b_ref, gk_ref, gqt_ref):
    seqlen = s_ref.shape[0]
    blk = SSM_CHUNK
    tri = _lower_tri(blk).astype(BF16)
    slot = lax.broadcasted_iota(jnp.int32, (blk, LANES), 1) % 8
    carry = jnp.zeros((1, LANES), F32)
    for c in range(seqlen // blk):
        rows = slice(c * blk, (c + 1) * blk)
        raw = s_ref[rows, :] + b_ref[...]
        logf = jnp.where(slot == F_LANE, -_softplus(-raw), 0.0)
        f_blk = _tri_cumsum(tri, logf) + carry
        carry = f_blk[blk - 1:blk, :]
        hi, mid, lo = (p.astype(F32) for p in _split3(f_blk * LOG2E))
        gq = hi + pltpu.roll(mid, 1, axis=1) + pltpu.roll(lo, 2, axis=1) + jnp.where((slot >= 3) & (slot < 6), 1.0, 0.0)
        gk = jnp.where(slot < 3, 1.0, 0.0) - (pltpu.roll(hi, 3, axis=1) + pltpu.roll(mid, 4, axis=1)
                                              + pltpu.roll(lo, 5, axis=1))
        gk_ref[rows, :] = gk.astype(BF16)
        gqt_ref[:, rows] = gq.T.astype(BF16)


def _fcumsum(small, fb_vec, bsz, seqlen):
    return pl.pallas_call(
        _fcumsum_kernel,
        grid=(bsz,),
        in_specs=[
            pl.BlockSpec((seqlen, LANES), lambda b: (b, 0)),
            pl.BlockSpec((1, LANES), lambda b: (0, 0)),
        ],
        out_specs=[
            pl.BlockSpec((seqlen, LANES), lambda b: (b, 0)),
            pl.BlockSpec((LANES, seqlen), lambda b: (b, 0)),
        ],
        out_shape=[
            jax.ShapeDtypeStruct((bsz * seqlen, LANES), BF16),
            jax.ShapeDtypeStruct((bsz * LANES, seqlen), BF16),
        ],
        compiler_params=pltpu.CompilerParams(
            dimension_semantics=("arbitrary",), vmem_limit_bytes=VMEM_LIMIT),
        name="fcumsum",
    )(small, fb_vec)


def _attn_kernel(qt_ref, k_ref, gk_ref, vta_ref, vtb_ref, gqt_ref, o_ref,
                 qaug_ref, s_a0, s_a1, s_b0, s_b1, acc_ref, m_ref, *, tq):
    hp = pl.program_id(1)
    i = pl.program_id(2)
    tk = KV_SUB
    sub_per_q = tq // tk
    s_refs = ((s_a0, s_a1), (s_b0, s_b1))
    vt_refs = (vta_ref, vtb_ref)

    grp = lax.broadcasted_iota(jnp.int32, (LANES, tq), 0) // 8
    gq_t = gqt_ref[...].astype(F32)
    zeros_half = jnp.zeros((HEAD_DIM, tq), BF16)
    qaug_ref[0, :HEAD_DIM] = qt_ref[:HEAD_DIM, :]
    qaug_ref[0, HEAD_DIM:LANES] = zeros_half
    qaug_ref[1, :HEAD_DIM] = zeros_half
    qaug_ref[1, HEAD_DIM:LANES] = qt_ref[HEAD_DIM:, :]
    for h in range(2):
        qaug_ref[h, LANES:] = jnp.where(grp == 2 * hp + h, gq_t, 0.0).astype(BF16)
    acc_ref[...] = jnp.zeros(acc_ref.shape, F32)
    m_ref[...] = jnp.full(m_ref.shape, -jnp.inf, F32)

    def qk(sub, n0, slot):
        rows = pl.ds(pl.multiple_of(sub * tk, tk), tk)
        k_sub = jnp.concatenate([k_ref[rows, :], gk_ref[rows, :]], axis=1)
        for h in range(2):
            s_refs[h][slot][:, n0:] = jnp.dot(k_sub, qaug_ref[h, :, n0:], preferred_element_type=F32)

    def update(sub, n0, slot, keep=None):
        for h in range(2):
            s = s_refs[h][slot][:, n0:]
            if keep is not None:
                diag = jnp.where(keep, s[:, :tk], -jnp.inf)
                s = diag if n0 + tk == tq else jnp.concatenate([diag, s[:, tk:]], axis=1)
            m = m_ref[h, 0:1, n0:]
            mn = jnp.maximum(m, jnp.max(s, axis=0, keepdims=True))
            p = jnp.exp2(s - mn).astype(BF16)
            acc_ref[h, :, n0:] = (acc_ref[h, :, n0:] * jnp.exp2(m - mn)
                                  + jnp.dot(vt_refs[h][sub], p, preferred_element_type=F32))
            m_ref[h, 0:1, n0:] = mn

    first_diag = i * sub_per_q
    qk(0, 0, 0)

    def body(j, carry):
        for c in range(sub_per_q):
            sub = j * sub_per_q + c
            qk(sub + 1, 0, (c + 1) % 2)
            update(sub, 0, c % 2)
        return carry

    lax.fori_loop(0, i, body, 0)

    keep = lax.broadcasted_iota(jnp.int32, (tk, tk), 0) <= lax.broadcasted_iota(jnp.int32, (tk, tk), 1)
    for c in range(sub_per_q):
        n0 = c * tk
        if c + 1 < sub_per_q:
            qk(first_diag + c + 1, n0 + tk, (c + 1) % 2)
        update(first_diag + c, n0, c % 2, keep)

    inv_a = 1.0 / acc_ref[0, HEAD_DIM:HEAD_DIM + 1, :]
    inv_b = 1.0 / acc_ref[1, 0:1, :]
    o_t = jnp.concatenate([acc_ref[0, :HEAD_DIM, :] * inv_a, acc_ref[1, HEAD_DIM:, :] * inv_b], axis=0)
    o_ref[...] = o_t.T.astype(BF16)


def _attention(proj, q_t, vta, vtb, gk, gq_t, bsz, seqlen, d_model, tq):
    assert (tq // KV_SUB) % 2 == 0, "score buffers alternate per key sub-tile"
    n_hp = d_model // LANES
    per_row = d_model // LANES
    nq = seqlen // tq
    n_sub = seqlen // KV_SUB
    score_buf = pltpu.VMEM((KV_SUB, tq), F32)
    return pl.pallas_call(
        functools.partial(_attn_kernel, tq=tq),
        grid=(bsz, n_hp, nq),
        in_specs=[
            pl.BlockSpec((LANES, tq), lambda b, h, i: (h, b * nq + i)),
            pl.BlockSpec((seqlen, LANES), lambda b, h, i: (b, COL_K * per_row + h)),
            pl.BlockSpec((seqlen, LANES), lambda b, h, i: (b, 0)),
            pl.BlockSpec((n_sub, LANES, KV_SUB), lambda b, h, i: (b, h, 0)),
            pl.BlockSpec((n_sub, LANES, KV_SUB), lambda b, h, i: (b, h, 0)),
            pl.BlockSpec((LANES, tq), lambda b, h, i: (b, i)),
        ],
        out_specs=pl.BlockSpec((tq, LANES), lambda b, h, i: (b * nq + i, h)),
        out_shape=jax.ShapeDtypeStruct((bsz * seqlen, d_model), BF16),
        scratch_shapes=[
            pltpu.VMEM((2, 2 * LANES, tq), BF16),
            score_buf, score_buf, score_buf, score_buf,
            pltpu.VMEM((2, LANES, tq), F32),
            pltpu.VMEM((2, 8, tq), F32),
        ],
        compiler_params=pltpu.CompilerParams(
            dimension_semantics=("arbitrary", "arbitrary", "arbitrary"), vmem_limit_bytes=VMEM_LIMIT),
        name="fox_attention",
    )(q_t, proj, gk, vta, vtb, gq_t)


def _ssd_chunk(c, xs_ref, bc_ref, s_ref, cw_ref, cb_ref, dtb_ref, alog_ref, dskip_ref,
               expand_ref, shift_ref, y_ref, prev_ref, st_ref, n_heads):
    q = SSM_CHUNK
    d_inner = xs_ref.shape[1]
    gn = SSM_GROUPS * SSM_STATE
    heads_per_group = n_heads // SSM_GROUPS
    gw = heads_per_group * HEAD_DIM

    @pl.when(c == 0)
    def _():
        prev_ref[...] = jnp.zeros(prev_ref.shape, BF16)
        st_ref[...] = jnp.zeros(st_ref.shape, F32)

    cur = jnp.concatenate([xs_ref[...], bc_ref[...]], axis=1)
    both = jnp.concatenate([cur, prev_ref[...]], axis=0)
    prev_ref[...] = cur
    acts = []
    for n in range(cur.shape[1] // CONV_COLS):
        cols = slice(n * CONV_COLS, (n + 1) * CONV_COLS)
        shifted = jnp.dot(shift_ref[...], both[:, cols], preferred_element_type=F32)
        conv = cb_ref[:, cols] + cw_ref[SSM_CONV - 1:SSM_CONV, cols] * cur[:, cols].astype(F32)
        for jj in range(SSM_CONV - 1):
            k = SSM_CONV - 1 - jj
            conv = conv + cw_ref[jj:jj + 1, cols] * shifted[(k - 1) * q:k * q, :]
        acts.append(_silu(conv))
        yield
    act = jnp.concatenate(acts, axis=1)
    xs = act[:, :d_inner]
    b16 = act[:, d_inner:d_inner + gn].astype(BF16)
    c16 = act[:, d_inner + gn:].astype(BF16)

    lane = lax.broadcasted_iota(jnp.int32, (q, LANES), 1)
    head_lane = lane % 8 == DT_LANE
    dt = jnp.where(head_lane, _softplus(s_ref[...] + dtb_ref[...]), 0.0)
    da = dt * (-jnp.exp(alog_ref[...]))
    tri = _lower_tri(q)
    cs = _tri_cumsum(tri.astype(BF16), da)
    yield
    cs2 = cs * LOG2E
    key_t = (cs2 - jnp.log2(dt)).T
    last = cs[q - 1:q, :]
    w_end = dt * jnp.exp(last - cs)
    ecs = jnp.exp(cs)
    yield

    def expand(v):
        hi, mid, _ = _split3(jnp.where(head_lane, v, 0.0))
        both = hi.astype(F32) + pltpu.roll(mid.astype(F32), 1, axis=1)
        return jnp.dot(both.astype(BF16), expand_ref[...], preferred_element_type=F32)

    w_e = expand(w_end)
    ecs_e = expand(ecs)
    xw16 = (xs * w_e).astype(BF16)
    yield

    xs16 = xs.astype(BF16)
    first_half = lax.broadcasted_iota(jnp.int32, (q, LANES), 1) < HEAD_DIM

    y_parts = []
    for g in range(SSM_GROUPS):
        bg = b16[:, g * SSM_STATE:(g + 1) * SSM_STATE]
        cg = c16[:, g * SSM_STATE:(g + 1) * SSM_STATE]
        cb = lax.dot_general(cg, bg, (((1,), (1,)), ((), ())), preferred_element_type=F32)
        st_prev = st_ref[g]
        y_off = jnp.dot(cg, st_prev.astype(BF16), preferred_element_type=F32)
        yield
        for pair in range(heads_per_group // 2):
            lo_lane = g * gw + pair * LANES
            halves = []
            for half in range(2):
                h = 8 * (g * heads_per_group + pair * 2 + half) + DT_LANE
                seg = cs2[:, h:h + 1] - key_t[h:h + 1, :]
                mmat = (cb * jnp.exp2(jnp.where(tri, seg, -jnp.inf))).astype(BF16)
                halves.append(jnp.dot(mmat, xs16[:, lo_lane:lo_lane + LANES], preferred_element_type=F32))
            y_pair = jnp.where(first_half, halves[0], halves[1])
            y_pair = y_pair + y_off[:, pair * LANES:(pair + 1) * LANES] * ecs_e[:, lo_lane:lo_lane + LANES]
            y_parts.append(y_pair)
            yield
        bg_t = bg.astype(F32).T.astype(BF16)
        new_st = jnp.dot(bg_t, xw16[:, g * gw:(g + 1) * gw], preferred_element_type=F32)
        st_ref[g] = st_prev * ecs_e[q - 1:q, g * gw:(g + 1) * gw] + new_st
        yield

    y_ref[...] = (jnp.concatenate(y_parts, axis=1) + xs * dskip_ref[...]).astype(BF16)
    yield


def _ssd_kernel(xs_ref, bc_ref, s_ref, cw_ref, cb_ref, dtb_ref, alog_ref, dskip_ref,
                expand_ref, shift_ref, y_ref, prev_ref, st_ref, *, n_heads):
    c = pl.program_id(1)
    chunks = [_ssd_chunk(c, xs_ref.at[seq], bc_ref.at[seq], s_ref.at[seq], cw_ref, cb_ref, dtb_ref, alog_ref,
                         dskip_ref, expand_ref, shift_ref, y_ref.at[seq], prev_ref.at[seq], st_ref.at[seq], n_heads)
              for seq in range(xs_ref.shape[0])]
    for _ in zip(*chunks):
        pass


def _ssd(proj, small, conv_w, conv_b, dtb_vec, alog_vec, dskip_e, expand_m, shift_m, bsz, seqlen, d_model, n_heads):
    nc = seqlen // SSM_CHUNK
    q = SSM_CHUNK
    conv_dim = conv_w.shape[1]
    gw = (n_heads // SSM_GROUPS) * HEAD_DIM
    nb = 2 if bsz % 2 == 0 else 1
    proj3 = proj.reshape(bsz, seqlen, proj.shape[1])
    const = lambda b, c: (0, 0)
    y = pl.pallas_call(
        functools.partial(_ssd_kernel, n_heads=n_heads),
        grid=(bsz // nb, nc),
        in_specs=[
            pl.BlockSpec((nb, q, d_model), lambda b, c: (b, c, COL_XS)),
            pl.BlockSpec((nb, q, d_model), lambda b, c: (b, c, COL_BC)),
            pl.BlockSpec((nb, q, LANES), lambda b, c: (b, c, 0)),
            pl.BlockSpec((SSM_CONV, conv_dim), const),
            pl.BlockSpec((1, conv_dim), const),
            pl.BlockSpec((1, LANES), const),
            pl.BlockSpec((1, LANES), const),
            pl.BlockSpec((1, d_model), const),
            pl.BlockSpec((LANES, d_model), const),
            pl.BlockSpec(((SSM_CONV - 1) * q, 2 * q), const),
        ],
        out_specs=pl.BlockSpec((nb, q, d_model), lambda b, c: (b, c, 0)),
        out_shape=jax.ShapeDtypeStruct((bsz, seqlen, d_model), BF16),
        scratch_shapes=[
            pltpu.VMEM((nb, q, conv_dim), BF16),
            pltpu.VMEM((nb, SSM_GROUPS, SSM_STATE, gw), F32),
        ],
        compiler_params=pltpu.CompilerParams(
            dimension_semantics=("arbitrary", "arbitrary"), vmem_limit_bytes=VMEM_LIMIT),
        name="ssd_branch",
    )(proj3, proj3, small.reshape(bsz, seqlen, LANES), conv_w, conv_b, dtb_vec, alog_vec, dskip_e, expand_m, shift_m)
    return y.reshape(bsz * seqlen, d_model)


def _half_silu_gate(v16, zh16):
    return v16 * zh16 * (jnp.tanh(zh16) + 1.0)


def _merge_kernel(ys_ref, zs_ref, oa_ref, za_ref, gs_ref, ga_ref, x_ref, ws_ref, wa_ref, wo_ref, pg_ref, o_ref):
    yg = _half_silu_gate(ys_ref[...], zs_ref[...]).astype(F32)
    gw = yg.shape[1] // SSM_GROUPS
    normed = []
    for g in range(SSM_GROUPS):
        blk = yg[:, g * gw:(g + 1) * gw]
        normed.append(blk * lax.rsqrt(jnp.mean(blk * blk, axis=-1, keepdims=True) + RMS_EPS))
    y_ssm = jnp.concatenate(normed, axis=1).astype(BF16)
    y_att = _half_silu_gate(oa_ref[...], za_ref[...])
    p_ssm = jnp.dot(y_ssm, ws_ref[...], preferred_element_type=F32)
    p_att = jnp.dot(y_att, wa_ref[...], preferred_element_type=F32)
    gate_s = (jnp.tanh(gs_ref[...]) + 1.0).astype(F32)
    gate_a = (jnp.tanh(ga_ref[...]) + 1.0).astype(F32)
    merged = gate_s * p_ssm + gate_a * p_att
    out = jnp.dot(merged.astype(BF16), wo_ref[...], preferred_element_type=F32)
    ms = jnp.mean(out * out, axis=-1, keepdims=True)
    o_ref[...] = x_ref[...] + out * lax.rsqrt(ms + RMS_EPS) * pg_ref[...]


def _merge(y_scan, o_att, proj, x2, w_ssm, w_att, w_out, post_g, tm):
    m, d = x2.shape
    row = lambda i: (i, 0)
    full = lambda i: (0, 0)
    return pl.pallas_call(
        _merge_kernel,
        grid=(m // tm,),
        in_specs=[
            pl.BlockSpec((tm, d), row),
            pl.BlockSpec((tm, d), lambda i: (i, COL_Z_SSM)),
            pl.BlockSpec((tm, d), row),
            pl.BlockSpec((tm, d), lambda i: (i, COL_Z_ATT)),
            pl.BlockSpec((tm, d), lambda i: (i, COL_G_SSM)),
            pl.BlockSpec((tm, d), lambda i: (i, COL_G_ATT)),
            pl.BlockSpec((tm, d), row),
            pl.BlockSpec((d, d), full),
            pl.BlockSpec((d, d), full),
            pl.BlockSpec((d, d), full),
            pl.BlockSpec((1, d), full),
        ],
        out_specs=pl.BlockSpec((tm, d), row),
        out_shape=jax.ShapeDtypeStruct((m, d), F32),
        compiler_params=pltpu.CompilerParams(
            dimension_semantics=("arbitrary",), vmem_limit_bytes=VMEM_LIMIT),
        name="gated_merge",
    )(y_scan, proj, o_att, proj, proj, proj, x2, w_ssm, w_att, w_out, post_g)


def _head_lanes(v, slot):
    cols = [jnp.zeros_like(v, dtype=F32)] * 8
    cols[slot] = v.astype(F32)
    return jnp.stack(cols, axis=-1).reshape(1, LANES)


def _layer(x, pre_g, w_in, conv_w, conv_b, dt_bias, a_log, d_skip, ssm_norm_g, fgate_b,
           w_branch_ssm, w_branch_att, w_out, post_g):
    bsz, seqlen, d = x.shape
    n_heads = d // HEAD_DIM
    assert n_heads * 8 == LANES, "small-projection lane layout assumes 16 heads"
    gn = SSM_GROUPS * SSM_STATE
    sizes = (d, d + 2 * gn, n_heads, d, d, d, n_heads, d, d, d)
    offs = [0]
    for sz in sizes:
        offs.append(offs[-1] + sz)
    z_ssm, xbc, dt_w, q_w, k_w, v_w, f_w, z_att, g_ssm, g_att = (w_in[:, offs[n]:offs[n + 1]] for n in range(10))
    w_big = jnp.concatenate([0.5 * z_ssm, xbc, k_w, 0.5 * z_att, 0.5 * g_ssm, 0.5 * g_att], axis=1).astype(BF16)
    w_t = jnp.concatenate([(q_w * (HEAD_DIM ** -0.5 * LOG2E)).T, v_w.T], axis=0).astype(BF16)
    slots = [jnp.zeros_like(f_w)] * 8
    slots[F_LANE], slots[DT_LANE] = f_w, dt_w
    w_small = jnp.stack(slots, axis=-1).reshape(d, LANES).astype(BF16)

    x2 = x.reshape(bsz * seqlen, d)
    proj, small, q_t, vta, vtb = _in_proj(x2, pre_g.reshape(1, d), w_big, w_t, w_small, tm=min(1024, bsz * seqlen))

    gk, gq_t = _fcumsum(small, _head_lanes(fgate_b, F_LANE), bsz, seqlen)
    o_att = _attention(proj, q_t, vta, vtb, gk, gq_t, bsz, seqlen, d, tq=min(2048, seqlen))

    head_of_col = jnp.arange(d) // HEAD_DIM
    lane_ids = jnp.arange(LANES)[:, None]
    expand_m = ((lane_ids == 8 * head_of_col[None, :] + DT_LANE)
                | (lane_ids == 8 * head_of_col[None, :] + DT_LANE + 1)).astype(BF16)
    t_ids = jnp.arange(SSM_CHUNK)[:, None]
    src = jnp.arange(2 * SSM_CHUNK)[None, :]
    shift_m = jnp.concatenate([(src == jnp.where(t_ids >= k, t_ids - k, 2 * SSM_CHUNK + t_ids - k))
                               for k in range(1, SSM_CONV)], axis=0).astype(BF16)
    y_scan = _ssd(proj, small, conv_w, conv_b.reshape(1, -1), _head_lanes(dt_bias, DT_LANE), _head_lanes(a_log, DT_LANE),
                 jnp.repeat(d_skip.astype(F32), HEAD_DIM).reshape(1, d), expand_m, shift_m,
                 bsz, seqlen, d, n_heads)

    out = _merge(y_scan, o_att, proj, x2, (ssm_norm_g.astype(F32)[:, None] * w_branch_ssm).astype(BF16),
                 w_branch_att.astype(BF16), (0.5 * w_out).astype(BF16), post_g.reshape(1, d), tm=min(512, bsz * seqlen))
    return out.reshape(bsz, seqlen, d)


def kernel(x, pre_norm_g, w_in, conv_w, conv_b, dt_bias, a_log, d_skip, ssm_norm_g, fgate_b,
           w_branch_ssm, w_branch_att, w_out, post_norm_g):
    for i in range(pre_norm_g.shape[0]):
        x = _layer(x, pre_norm_g[i], w_in[i], conv_w[i], conv_b[i], dt_bias[i], a_log[i], d_skip[i],
                   ssm_norm_g[i], fgate_b[i], w_branch_ssm[i], w_branch_att[i], w_out[i], post_norm_g[i])
    return x
```

```python
import functools

import jax
import jax.numpy as jnp
import numpy as np
from jax import lax
from jax.experimental import pallas as pl
from jax.experimental.pallas import tpu as pltpu

F32 = jnp.float32
BF16 = jnp.bfloat16

RMS_EPS = 1e-6
HEAD_DIM = 64
SSM_GROUPS = 4
SSM_STATE = 128
SSM_CONV = 4
SSM_CHUNK = 128
CONV_COLS = 512
NORM_ROWS = 256
KV_SUB = 256
LANES = 128
VMEM_LIMIT = 48 * 1024 * 1024

COL_Z_SSM, COL_XS, COL_BC, COL_K, COL_Z_ATT, COL_G_SSM, COL_G_ATT = range(7)
N_COL_BLOCKS = 7
N_T_BLOCKS = 2
LOG2E = 1.4426950408889634
F_LANE = 0
DT_LANE = 4


def _sigmoid(x):
    return 0.5 * jnp.tanh(0.5 * x) + 0.5


def _silu(x):
    return x * _sigmoid(x)


def _softplus(x):
    return jnp.maximum(x, 0.0) + jnp.log(1.0 + jnp.exp(-jnp.abs(x)))


def _split3(x):
    hi = x.astype(BF16)
    r1 = x - hi.astype(F32)
    mid = r1.astype(BF16)
    lo = (r1 - mid.astype(F32)).astype(BF16)
    return hi, mid, lo


def _tri_cumsum(tri_bf16, x):
    hi, mid, lo = _split3(x)
    dot = functools.partial(jnp.dot, preferred_element_type=F32)
    return dot(tri_bf16, hi) + dot(tri_bf16, mid) + dot(tri_bf16, lo)


def _lower_tri(n):
    r = lax.broadcasted_iota(jnp.int32, (n, n), 0)
    c = lax.broadcasted_iota(jnp.int32, (n, n), 1)
    return r >= c


def _in_proj_kernel(x_ref, g_ref, w_ref, wt_ref, ws_ref, p_ref, s_ref, qt_ref, vta_ref, vtb_ref, h_ref):
    j = pl.program_id(1)
    nt = (((1,), (1,)), ((), ()))

    @pl.when(j == 0)
    def _():
        for r0 in range(0, x_ref.shape[0], NORM_ROWS):
            rows = slice(r0, r0 + NORM_ROWS)
            x = x_ref[rows, :]
            ms = jnp.mean(x * x, axis=-1, keepdims=True)
            h = (x * lax.rsqrt(ms + RMS_EPS) * g_ref[...]).astype(BF16)
            h_ref[rows, :] = h
            s_ref[rows, :] = jnp.dot(h, ws_ref[...], preferred_element_type=F32)
            p_ref[rows, :] = jnp.dot(h, w_ref[...], preferred_element_type=F32).astype(BF16)

    @pl.when((j > 0) & (j < N_COL_BLOCKS))
    def _():
        p_ref[...] = jnp.dot(h_ref[...], w_ref[...], preferred_element_type=F32).astype(BF16)

    @pl.when(j == N_COL_BLOCKS)
    def _():
        qt_ref[...] = lax.dot_general(wt_ref[...], h_ref[...], nt, preferred_element_type=F32).astype(BF16)

    @pl.when(j == N_COL_BLOCKS + 1)
    def _():
        vt = lax.dot_general(wt_ref[...], h_ref[...], nt, preferred_element_type=F32)
        first = (lax.broadcasted_iota(jnp.int32, vt.shape, 0) % LANES) < HEAD_DIM
        vta = jnp.where(first, vt, 1.0).astype(BF16)
        vtb = jnp.where(first, 1.0, vt).astype(BF16)
        for c in range(vta_ref.shape[0]):
            vta_ref[c] = vta[:, c * KV_SUB:(c + 1) * KV_SUB]
            vtb_ref[c] = vtb[:, c * KV_SUB:(c + 1) * KV_SUB]


def _in_proj(x2, g, w_big, w_t, w_small, tm):
    m, d = x2.shape
    last = N_COL_BLOCKS - 1
    sub = tm // KV_SUB
    vt_shape = jax.ShapeDtypeStruct((m // KV_SUB, d, KV_SUB), BF16)
    return pl.pallas_call(
        _in_proj_kernel,
        grid=(m // tm, N_COL_BLOCKS + N_T_BLOCKS),
        in_specs=[
            pl.BlockSpec((tm, d), lambda i, j: (i, 0)),
            pl.BlockSpec((1, d), lambda i, j: (0, 0)),
            pl.BlockSpec((d, d), lambda i, j: (0, jnp.minimum(j, last))),
            pl.BlockSpec((d, d), lambda i, j: (jnp.maximum(j - N_COL_BLOCKS, 0), 0)),
            pl.BlockSpec((d, LANES), lambda i, j: (0, 0)),
        ],
        out_specs=[
            pl.BlockSpec((tm, d), lambda i, j: (i, jnp.minimum(j, last))),
            pl.BlockSpec((tm, LANES), lambda i, j: (i, 0)),
            pl.BlockSpec((d, tm), lambda i, j: (0, i)),
            pl.BlockSpec((sub, d, KV_SUB), lambda i, j: (i, 0, 0)),
            pl.BlockSpec((sub, d, KV_SUB), lambda i, j: (i, 0, 0)),
        ],
        out_shape=[
            jax.ShapeDtypeStruct((m, N_COL_BLOCKS * d), BF16),
            jax.ShapeDtypeStruct((m, LANES), F32),
            jax.ShapeDtypeStruct((d, m), BF16),
            vt_shape,
            vt_shape,
        ],
        scratch_shapes=[pltpu.VMEM((tm, d), BF16)],
        compiler_params=pltpu.CompilerParams(
            dimension_semantics=("arbitrary", "arbitrary"), vmem_limit_bytes=VMEM_LIMIT),
        name="in_proj",
    )(x2, g, w_big, w_t, w_small)


def _fcumsum_kernel(s_ref, b_ref, gk_ref, gqt_ref):
    seqlen = s_ref.shape[0]
    blk = SSM_CHUNK
    tri = _lower_tri(blk).astype(BF16)
    slot = lax.broadcasted_iota(jnp.int32, (blk, LANES), 1) % 8
    carry = jnp.zeros((1, LANES), F32)
    for c in range(seqlen // blk):
        rows = slice(c * blk, (c + 1) * blk)
        raw = s_ref[rows, :] + b_ref[...]
        logf = jnp.where(slot == F_LANE, -_softplus(-raw), 0.0)
        f_blk = _tri_cumsum(tri, logf) + carry
        carry = f_blk[blk - 1:blk, :]
        hi, mid, lo = (p.astype(F32) for p in _split3(f_blk * LOG2E))
        gq = hi + pltpu.roll(mid, 1, axis=1) + pltpu.roll(lo, 2, axis=1) + jnp.where((slot >= 3) & (slot < 6), 1.0, 0.0)
        gk = jnp.where(slot < 3, 1.0, 0.0) - (pltpu.roll(hi, 3, axis=1) + pltpu.roll(mid, 4, axis=1)
                                              + pltpu.roll(lo, 5, axis=1))
        gk_ref[rows, :] = gk.astype(BF16)
        gqt_ref[:, rows] = gq.T.astype(BF16)


def _fcumsum(small, fb_vec, bsz, seqlen):
    return pl.pallas_call(
        _fcumsum_kernel,
        grid=(bsz,),
        in_specs=[
            pl.BlockSpec((seqlen, LANES), lambda b: (b, 0)),
            pl.BlockSpec((1, LANES), lambda b: (0, 0)),
        ],
        out_specs=[
            pl.BlockSpec((seqlen, LANES), lambda b: (b, 0)),
            pl.BlockSpec((LANES, seqlen), lambda b: (b, 0)),
        ],
        out_shape=[
            jax.ShapeDtypeStruct((bsz * seqlen, LANES), BF16),
            jax.ShapeDtypeStruct((bsz * LANES, seqlen), BF16),
        ],
        compiler_params=pltpu.CompilerParams(
            dimension_semantics=("arbitrary",), vmem_limit_bytes=VMEM_LIMIT),
        name="fcumsum",
    )(small, fb_vec)


def _attn_kernel(qt_ref, k_ref, gk_ref, vta_ref, vtb_ref, gqt_ref, o_ref,
                 qaug_ref, s_a0, s_a1, s_b0, s_b1, acc_ref, m_ref, *, tq):
    hp = pl.program_id(1)
    i = pl.program_id(2)
    tk = KV_SUB
    sub_per_q = tq // tk
    s_refs = ((s_a0, s_a1), (s_b0, s_b1))
    vt_refs = (vta_ref, vtb_ref)

    grp = lax.broadcasted_iota(jnp.int32, (LANES, tq), 0) // 8
    gq_t = gqt_ref[...].astype(F32)
    zeros_half = jnp.zeros((HEAD_DIM, tq), BF16)
    qaug_ref[0, :HEAD_DIM] = qt_ref[:HEAD_DIM, :]
    qaug_ref[0, HEAD_DIM:LANES] = zeros_half
    qaug_ref[1, :HEAD_DIM] = zeros_half
    qaug_ref[1, HEAD_DIM:LANES] = qt_ref[HEAD_DIM:, :]
    for h in range(2):
        qaug_ref[h, LANES:] = jnp.where(grp == 2 * hp + h, gq_t, 0.0).astype(BF16)
    acc_ref[...] = jnp.zeros(acc_ref.shape, F32)
    m_ref[...] = jnp.full(m_ref.shape, -jnp.inf, F32)

    def qk(sub, n0, slot):
        rows = pl.ds(pl.multiple_of(sub * tk, tk), tk)
        k_sub = jnp.concatenate([k_ref[rows, :], gk_ref[rows, :]], axis=1)
        for h in range(2):
            s_refs[h][slot][:, n0:] = jnp.dot(k_sub, qaug_ref[h, :, n0:], preferred_element_type=F32)

    def update(sub, n0, slot, keep=None):
        for h in range(2):
            s = s_refs[h][slot][:, n0:]
            if keep is not None:
                diag = jnp.where(keep, s[:, :tk], -jnp.inf)
                s = diag if n0 + tk == tq else jnp.concatenate([diag, s[:, tk:]], axis=1)
            m = m_ref[h, 0:1, n0:]
            mn = jnp.maximum(m, jnp.max(s, axis=0, keepdims=True))
            p = jnp.exp2(s - mn).astype(BF16)
            acc_ref[h, :, n0:] = (acc_ref[h, :, n0:] * jnp.exp2(m - mn)
                                  + jnp.dot(vt_refs[h][sub], p, preferred_element_type=F32))
            m_ref[h, 0:1, n0:] = mn

    first_diag = i * sub_per_q
    qk(0, 0, 0)

    def body(j, carry):
        for c in range(sub_per_q):
            sub = j * sub_per_q + c
            qk(sub + 1, 0, (c + 1) % 2)
            update(sub, 0, c % 2)
        return carry

    lax.fori_loop(0, i, body, 0)

    keep = lax.broadcasted_iota(jnp.int32, (tk, tk), 0) <= lax.broadcasted_iota(jnp.int32, (tk, tk), 1)
    for c in range(sub_per_q):
        n0 = c * tk
        if c + 1 < sub_per_q:
            qk(first_diag + c + 1, n0 + tk, (c + 1) % 2)
        update(first_diag + c, n0, c % 2, keep)

    inv_a = 1.0 / acc_ref[0, HEAD_DIM:HEAD_DIM + 1, :]
    inv_b = 1.0 / acc_ref[1, 0:1, :]
    o_t = jnp.concatenate([acc_ref[0, :HEAD_DIM, :] * inv_a, acc_ref[1, HEAD_DIM:, :] * inv_b], axis=0)
    o_ref[...] = o_t.T.astype(BF16)


def _attention(proj, q_t, vta, vtb, gk, gq_t, bsz, seqlen, d_model, tq):
    assert (tq // KV_SUB) % 2 == 0, "score buffers alternate per key sub-tile"
    n_hp = d_model // LANES
    per_row = d_model // LANES
    nq = seqlen // tq
    n_sub = seqlen // KV_SUB
    score_buf = pltpu.VMEM((KV_SUB, tq), F32)
    return pl.pallas_call(
        functools.partial(_attn_kernel, tq=tq),
        grid=(bsz, n_hp, nq),
        in_specs=[
            pl.BlockSpec((LANES, tq), lambda b, h, i: (h, b * nq + i)),
            pl.BlockSpec((seqlen, LANES), lambda b, h, i: (b, COL_K * per_row + h)),
            pl.BlockSpec((seqlen, LANES), lambda b, h, i: (b, 0)),
            pl.BlockSpec((n_sub, LANES, KV_SUB), lambda b, h, i: (b, h, 0)),
            pl.BlockSpec((n_sub, LANES, KV_SUB), lambda b, h, i: (b, h, 0)),
            pl.BlockSpec((LANES, tq), lambda b, h, i: (b, i)),
        ],
        out_specs=pl.BlockSpec((tq, LANES), lambda b, h, i: (b * nq + i, h)),
        out_shape=jax.ShapeDtypeStruct((bsz * seqlen, d_model), BF16),
        scratch_shapes=[
            pltpu.VMEM((2, 2 * LANES, tq), BF16),
            score_buf, score_buf, score_buf, score_buf,
            pltpu.VMEM((2, LANES, tq), F32),
            pltpu.VMEM((2, 8, tq), F32),
        ],
        compiler_params=pltpu.CompilerParams(
            dimension_semantics=("arbitrary", "arbitrary", "arbitrary"), vmem_limit_bytes=VMEM_LIMIT),
        name="fox_attention",
    )(q_t, proj, gk, vta, vtb, gq_t)


def _ssd_chunk(c, xs_ref, bc_ref, s_ref, cw_ref, cb_ref, dtb_ref, alog_ref, dskip_ref,
               expand_ref, shift_ref, y_ref, prev_ref, st_ref, n_heads):
    q = SSM_CHUNK
    d_inner = xs_ref.shape[1]
    gn = SSM_GROUPS * SSM_STATE
    heads_per_group = n_heads // SSM_GROUPS
    gw = heads_per_group * HEAD_DIM

    @pl.when(c == 0)
    def _():
        prev_ref[...] = jnp.zeros(prev_ref.shape, BF16)
        st_ref[...] = jnp.zeros(st_ref.shape, F32)

    cur = jnp.concatenate([xs_ref[...], bc_ref[...]], axis=1)
    both = jnp.concatenate([cur, prev_ref[...]], axis=0)
    prev_ref[...] = cur
    acts = []
    for n in range(cur.shape[1] // CONV_COLS):
        cols = slice(n * CONV_COLS, (n + 1) * CONV_COLS)
        shifted = jnp.dot(shift_ref[...], both[:, cols], preferred_element_type=F32)
        conv = cb_ref[:, cols] + cw_ref[SSM_CONV - 1:SSM_CONV, cols] * cur[:, cols].astype(F32)
        for jj in range(SSM_CONV - 1):
            k = SSM_CONV - 1 - jj
            conv = conv + cw_ref[jj:jj + 1, cols] * shifted[(k - 1) * q:k * q, :]
        acts.append(_silu(conv))
        yield
    act = jnp.concatenate(acts, axis=1)
    xs = act[:, :d_inner]
    b16 = act[:, d_inner:d_inner + gn].astype(BF16)
    c16 = act[:, d_inner + gn:].astype(BF16)

    lane = lax.broadcasted_iota(jnp.int32, (q, LANES), 1)
    head_lane = lane % 8 == DT_LANE
    dt = jnp.where(head_lane, _softplus(s_ref[...] + dtb_ref[...]), 0.0)
    da = dt * (-jnp.exp(alog_ref[...]))
    tri = _lower_tri(q)
    cs = _tri_cumsum(tri.astype(BF16), da)
    yield
    cs2 = cs * LOG2E
    key_t = (cs2 - jnp.log2(dt)).T
    last = cs[q - 1:q, :]
    w_end = dt * jnp.exp(last - cs)
    ecs = jnp.exp(cs)
    yield

    def expand(v):
        hi, mid, _ = _split3(jnp.where(head_lane, v, 0.0))
        both = hi.astype(F32) + pltpu.roll(mid.astype(F32), 1, axis=1)
        return jnp.dot(both.astype(BF16), expand_ref[...], preferred_element_type=F32)

    w_e = expand(w_end)
    ecs_e = expand(ecs)
    xw16 = (xs * w_e).astype(BF16)
    yield

    xs16 = xs.astype(BF16)
    first_half = lax.broadcasted_iota(jnp.int32, (q, LANES), 1) < HEAD_DIM

    y_parts = []
    for g in range(SSM_GROUPS):
        bg = b16[:, g * SSM_STATE:(g + 1) * SSM_STATE]
        cg = c16[:, g * SSM_STATE:(g + 1) * SSM_STATE]
        cb = lax.dot_general(cg, bg, (((1,), (1,)), ((), ())), preferred_element_type=F32)
        st_prev = st_ref[g]
        y_off = jnp.dot(cg, st_prev.astype(BF16), preferred_element_type=F32)
        yield
        for pair in range(heads_per_group // 2):
            lo_lane = g * gw + pair * LANES
            halves = []
            for half in range(2):
                h = 8 * (g * heads_per_group + pair * 2 + half) + DT_LANE
                seg = cs2[:, h:h + 1] - key_t[h:h + 1, :]
                mmat = (cb * jnp.exp2(jnp.where(tri, seg, -jnp.inf))).astype(BF16)
                halves.append(jnp.dot(mmat, xs16[:, lo_lane:lo_lane + LANES], preferred_element_type=F32))
            y_pair = jnp.where(first_half, halves[0], halves[1])
            y_pair = y_pair + y_off[:, pair * LANES:(pair + 1) * LANES] * ecs_e[:, lo_lane:lo_lane + LANES]
            y_parts.append(y_pair)
            yield
        bg_t = bg.astype(F32).T.astype(BF16)
        new_st = jnp.dot(bg_t, xw16[:, g * gw:(g + 1) * gw], preferred_element_type=F32)
        st_ref[g] = st_prev * ecs_e[q - 1:q, g * gw:(g + 1) * gw] + new_st
        yield

    y_ref[...] = (jnp.concatenate(y_parts, axis=1) + xs * dskip_ref[...]).astype(BF16)
    yield


def _ssd_kernel(xs_ref, bc_ref, s_ref, cw_ref, cb_ref, dtb_ref, alog_ref, dskip_ref,
                expand_ref, shift_ref, y_ref, prev_ref, st_ref, *, n_heads):
    c = pl.program_id(1)
    chunks = [_ssd_chunk(c, xs_ref.at[seq], bc_ref.at[seq], s_ref.at[seq], cw_ref, cb_ref, dtb_ref, alog_ref,
                         dskip_ref, expand_ref, shift_ref, y_ref.at[seq], prev_ref.at[seq], st_ref.at[seq], n_heads)
              for seq in range(xs_ref.shape[0])]
    for _ in zip(*chunks):
        pass


def _ssd(proj, small, conv_w, conv_b, dtb_vec, alog_vec, dskip_e, expand_m, shift_m, bsz, seqlen, d_model, n_heads):
    nc = seqlen // SSM_CHUNK
    q = SSM_CHUNK
    conv_dim = conv_w.shape[1]
    gw = (n_heads // SSM_GROUPS) * HEAD_DIM
    nb = 2 if bsz % 2 == 0 else 1
    proj3 = proj.reshape(bsz, seqlen, proj.shape[1])
    const = lambda b, c: (0, 0)
    y = pl.pallas_call(
        functools.partial(_ssd_kernel, n_heads=n_heads),
        grid=(bsz // nb, nc),
        in_specs=[
            pl.BlockSpec((nb, q, d_model), lambda b, c: (b, c, COL_XS)),
            pl.BlockSpec((nb, q, d_model), lambda b, c: (b, c, COL_BC)),
            pl.BlockSpec((nb, q, LANES), lambda b, c: (b, c, 0)),
            pl.BlockSpec((SSM_CONV, conv_dim), const),
            pl.BlockSpec((1, conv_dim), const),
            pl.BlockSpec((1, LANES), const),
            pl.BlockSpec((1, LANES), const),
            pl.BlockSpec((1, d_model), const),
            pl.BlockSpec((LANES, d_model), const),
            pl.BlockSpec(((SSM_CONV - 1) * q, 2 * q), const),
        ],
        out_specs=pl.BlockSpec((nb, q, d_model), lambda b, c: (b, c, 0)),
        out_shape=jax.ShapeDtypeStruct((bsz, seqlen, d_model), BF16),
        scratch_shapes=[
            pltpu.VMEM((nb, q, conv_dim), BF16),
            pltpu.VMEM((nb, SSM_GROUPS, SSM_STATE, gw), F32),
        ],
        compiler_params=pltpu.CompilerParams(
            dimension_semantics=("arbitrary", "arbitrary"), vmem_limit_bytes=VMEM_LIMIT),
        name="ssd_branch",
    )(proj3, proj3, small.reshape(bsz, seqlen, LANES), conv_w, conv_b, dtb_vec, alog_vec, dskip_e, expand_m, shift_m)
    return y.reshape(bsz * seqlen, d_model)


def _half_silu_gate(v16, zh16):
    return v16 * zh16 * (jnp.tanh(zh16) + 1.0)


def _merge_kernel(ys_ref, zs_ref, oa_ref, za_ref, gs_ref, ga_ref, x_ref, ws_ref, wa_ref, wo_ref, pg_ref, o_ref):
    y_att = _half_silu_gate(oa_ref[...], za_ref[...])
    p_att = jnp.dot(y_att, wa_ref[...], preferred_element_type=F32)
    yg = _half_silu_gate(ys_ref[...], zs_ref[...]).astype(F32)
    gw = yg.shape[1] // SSM_GROUPS
    normed = []
    for g in range(SSM_GROUPS):
        blk = yg[:, g * gw:(g + 1) * gw]
        normed.append(blk * lax.rsqrt(jnp.mean(blk * blk, axis=-1, keepdims=True) + RMS_EPS))
    y_ssm = jnp.concatenate(normed, axis=1).astype(BF16)
    p_ssm = jnp.dot(y_ssm, ws_ref[...], preferred_element_type=F32)
    gate_s = (jnp.tanh(gs_ref[...]) + 1.0).astype(F32)
    gate_a = (jnp.tanh(ga_ref[...]) + 1.0).astype(F32)
    merged = gate_s * p_ssm + gate_a * p_att
    out = jnp.dot(merged.astype(BF16), wo_ref[...], preferred_element_type=F32)
    ms = jnp.mean(out * out, axis=-1, keepdims=True)
    o_ref[...] = x_ref[...] + out * lax.rsqrt(ms + RMS_EPS) * pg_ref[...]


def _merge(y_scan, o_att, proj, x2, w_ssm, w_att, w_out, post_g, tm):
    m, d = x2.shape
    row = lambda i: (i, 0)
    full = lambda i: (0, 0)
    return pl.pallas_call(
        _merge_kernel,
        grid=(m // tm,),
        in_specs=[
            pl.BlockSpec((tm, d), row),
            pl.BlockSpec((tm, d), lambda i: (i, COL_Z_SSM)),
            pl.BlockSpec((tm, d), row),
            pl.BlockSpec((tm, d), lambda i: (i, COL_Z_ATT)),
            pl.BlockSpec((tm, d), lambda i: (i, COL_G_SSM)),
            pl.BlockSpec((tm, d), lambda i: (i, COL_G_ATT)),
            pl.BlockSpec((tm, d), row),
            pl.BlockSpec((d, d), full),
            pl.BlockSpec((d, d), full),
            pl.BlockSpec((d, d), full),
            pl.BlockSpec((1, d), full),
        ],
        out_specs=pl.BlockSpec((tm, d), row),
        out_shape=jax.ShapeDtypeStruct((m, d), F32),
        compiler_params=pltpu.CompilerParams(
            dimension_semantics=("arbitrary",), vmem_limit_bytes=VMEM_LIMIT),
        name="gated_merge",
    )(y_scan, proj, o_att, proj, proj, proj, x2, w_ssm, w_att, w_out, post_g)


def _head_lanes(v, slot):
    cols = [jnp.zeros_like(v, dtype=F32)] * 8
    cols[slot] = v.astype(F32)
    return jnp.stack(cols, axis=-1).reshape(1, LANES)


def _layer(x, pre_g, w_in, conv_w, conv_b, dt_bias, a_log, d_skip, ssm_norm_g, fgate_b,
           w_branch_ssm, w_branch_att, w_out, post_g):
    bsz, seqlen, d = x.shape
    n_heads = d // HEAD_DIM
    assert n_heads * 8 == LANES, "small-projection lane layout assumes 16 heads"
    gn = SSM_GROUPS * SSM_STATE
    sizes = (d, d + 2 * gn, n_heads, d, d, d, n_heads, d, d, d)
    offs = [0]
    for sz in sizes:
        offs.append(offs[-1] + sz)
    z_ssm, xbc, dt_w, q_w, k_w, v_w, f_w, z_att, g_ssm, g_att = (w_in[:, offs[n]:offs[n + 1]] for n in range(10))
    w_big = jnp.concatenate([0.5 * z_ssm, xbc, k_w, 0.5 * z_att, 0.5 * g_ssm, 0.5 * g_att], axis=1).astype(BF16)
    w_t = jnp.concatenate([(q_w * (HEAD_DIM ** -0.5 * LOG2E)).T, v_w.T], axis=0).astype(BF16)
    slots = [jnp.zeros_like(f_w)] * 8
    slots[F_LANE], slots[DT_LANE] = f_w, dt_w
    w_small = jnp.stack(slots, axis=-1).reshape(d, LANES).astype(BF16)

    x2 = x.reshape(bsz * seqlen, d)
    proj, small, q_t, vta, vtb = _in_proj(x2, pre_g.reshape(1, d), w_big, w_t, w_small, tm=min(1024, bsz * seqlen))

    gk, gq_t = _fcumsum(small, _head_lanes(fgate_b, F_LANE), bsz, seqlen)
    o_att = _attention(proj, q_t, vta, vtb, gk, gq_t, bsz, seqlen, d, tq=min(2048, seqlen))

    head_of_col = np.arange(d) // HEAD_DIM
    lane_ids = np.arange(LANES)[:, None]
    expand_m = jnp.asarray((lane_ids == 8 * head_of_col[None, :] + DT_LANE)
                           | (lane_ids == 8 * head_of_col[None, :] + DT_LANE + 1), BF16)
    t_ids = np.arange(SSM_CHUNK)[:, None]
    src = np.arange(2 * SSM_CHUNK)[None, :]
    shift_m = jnp.asarray(np.concatenate([src == np.where(t_ids >= k, t_ids - k, 2 * SSM_CHUNK + t_ids - k)
                                          for k in range(1, SSM_CONV)], axis=0), BF16)
    y_scan = _ssd(proj, small, conv_w, conv_b.reshape(1, -1), _head_lanes(dt_bias, DT_LANE), _head_lanes(a_log, DT_LANE),
                 jnp.repeat(d_skip.astype(F32), HEAD_DIM).reshape(1, d), expand_m, shift_m,
                 bsz, seqlen, d, n_heads)

    out = _merge(y_scan, o_att, proj, x2, (ssm_norm_g.astype(F32)[:, None] * w_branch_ssm).astype(BF16),
                 w_branch_att.astype(BF16), (0.5 * w_out).astype(BF16), post_g.reshape(1, d), tm=min(512, bsz * seqlen))
    return out.reshape(bsz, seqlen, d)


def kernel(x, pre_norm_g, w_in, conv_w, conv_b, dt_bias, a_log, d_skip, ssm_norm_g, fgate_b,
           w_branch_ssm, w_branch_att, w_out, post_norm_g):
    for i in range(pre_norm_g.shape[0]):
        x = _layer(x, pre_norm_g[i], w_in[i], conv_w[i], conv_b[i], dt_bias[i], a_log[i], d_skip[i],
                   ssm_norm_g[i], fgate_b[i], w_branch_ssm[i], w_branch_att[i], w_out[i], post_norm_g[i])
    return x
```

```python
import functools

import jax
import jax.numpy as jnp
import numpy as np
from jax import lax
from jax.experimental import pallas as pl
from jax.experimental.pallas import tpu as pltpu

F32 = jnp.float32
BF16 = jnp.bfloat16

RMS_EPS = 1e-6
HEAD_DIM = 64
SSM_GROUPS = 4
SSM_STATE = 128
SSM_CONV = 4
SSM_CHUNK = 128
CONV_COLS = 512
NORM_ROWS = 256
KV_SUB = 256
LANES = 128
VMEM_LIMIT = 48 * 1024 * 1024

COL_Z_SSM, COL_XS, COL_BC, COL_K, COL_Z_ATT, COL_G_SSM, COL_G_ATT = range(7)
N_COL_BLOCKS = 7
N_T_BLOCKS = 2
LOG2E = 1.4426950408889634
F_LANE = 0
DT_LANE = 4


def _sigmoid(x):
    return 0.5 * jnp.tanh(0.5 * x) + 0.5


def _silu(x):
    return x * _sigmoid(x)


def _softplus(x):
    return jnp.maximum(x, 0.0) + jnp.log(1.0 + jnp.exp(-jnp.abs(x)))


def _split3(x):
    hi = x.astype(BF16)
    r1 = x - hi.astype(F32)
    mid = r1.astype(BF16)
    lo = (r1 - mid.astype(F32)).astype(BF16)
    return hi, mid, lo


def _tri_cumsum(tri_bf16, x):
    hi, mid, lo = _split3(x)
    dot = functools.partial(jnp.dot, preferred_element_type=F32)
    return dot(tri_bf16, hi) + dot(tri_bf16, mid) + dot(tri_bf16, lo)


def _lower_tri(n):
    r = lax.broadcasted_iota(jnp.int32, (n, n), 0)
    c = lax.broadcasted_iota(jnp.int32, (n, n), 1)
    return r >= c


def _in_proj_kernel(x_ref, g_ref, w_ref, wt_ref, ws_ref, p_ref, s_ref, qt_ref, vt_ref, h_ref):
    j = pl.program_id(1)
    nt = (((1,), (1,)), ((), ()))

    @pl.when(j == 0)
    def _():
        for r0 in range(0, x_ref.shape[0], NORM_ROWS):
            rows = slice(r0, r0 + NORM_ROWS)
            x = x_ref[rows, :]
            ms = jnp.mean(x * x, axis=-1, keepdims=True)
            h = (x * lax.rsqrt(ms + RMS_EPS) * g_ref[...]).astype(BF16)
            h_ref[rows, :] = h
            s_ref[rows, :] = jnp.dot(h, ws_ref[...], preferred_element_type=F32)
            p_ref[rows, :] = jnp.dot(h, w_ref[...], preferred_element_type=F32).astype(BF16)

    @pl.when((j > 0) & (j < N_COL_BLOCKS))
    def _():
        p_ref[...] = jnp.dot(h_ref[...], w_ref[...], preferred_element_type=F32).astype(BF16)

    @pl.when(j == N_COL_BLOCKS)
    def _():
        qt_ref[...] = lax.dot_general(wt_ref[...], h_ref[...], nt, preferred_element_type=F32).astype(BF16)

    @pl.when(j == N_COL_BLOCKS + 1)
    def _():
        vt = lax.dot_general(wt_ref[...], h_ref[...], nt, preferred_element_type=F32).astype(BF16)
        for c in range(vt_ref.shape[0]):
            vt_ref[c] = vt[:, c * KV_SUB:(c + 1) * KV_SUB]


def _in_proj(x2, g, w_big, w_t, w_small, tm):
    m, d = x2.shape
    last = N_COL_BLOCKS - 1
    sub = tm // KV_SUB
    vt_shape = jax.ShapeDtypeStruct((m // KV_SUB, d, KV_SUB), BF16)
    return pl.pallas_call(
        _in_proj_kernel,
        grid=(m // tm, N_COL_BLOCKS + N_T_BLOCKS),
        in_specs=[
            pl.BlockSpec((tm, d), lambda i, j: (i, 0)),
            pl.BlockSpec((1, d), lambda i, j: (0, 0)),
            pl.BlockSpec((d, d), lambda i, j: (0, jnp.minimum(j, last))),
            pl.BlockSpec((d, d), lambda i, j: (jnp.maximum(j - N_COL_BLOCKS, 0), 0)),
            pl.BlockSpec((d, LANES), lambda i, j: (0, 0)),
        ],
        out_specs=[
            pl.BlockSpec((tm, d), lambda i, j: (i, jnp.minimum(j, last))),
            pl.BlockSpec((tm, LANES), lambda i, j: (i, 0)),
            pl.BlockSpec((d, tm), lambda i, j: (0, i)),
            pl.BlockSpec((sub, d, KV_SUB), lambda i, j: (i, 0, 0)),
        ],
        out_shape=[
            jax.ShapeDtypeStruct((m, N_COL_BLOCKS * d), BF16),
            jax.ShapeDtypeStruct((m, LANES), F32),
            jax.ShapeDtypeStruct((d, m), BF16),
            vt_shape,
        ],
        scratch_shapes=[pltpu.VMEM((tm, d), BF16)],
        compiler_params=pltpu.CompilerParams(
            dimension_semantics=("arbitrary", "arbitrary"), vmem_limit_bytes=VMEM_LIMIT),
        name="in_proj",
    )(x2, g, w_big, w_t, w_small)


def _fcumsum_kernel(s_ref, b_ref, gk_ref, gqt_ref):
    seqlen = s_ref.shape[0]
    blk = SSM_CHUNK
    tri = _lower_tri(blk).astype(BF16)
    slot = lax.broadcasted_iota(jnp.int32, (blk, LANES), 1) % 8
    carry = jnp.zeros((1, LANES), F32)
    for c in range(seqlen // blk):
        rows = slice(c * blk, (c + 1) * blk)
        raw = s_ref[rows, :] + b_ref[...]
        logf = jnp.where(slot == F_LANE, -_softplus(-raw), 0.0)
        f_blk = _tri_cumsum(tri, logf) + carry
        carry = f_blk[blk - 1:blk, :]
        hi, mid, lo = (p.astype(F32) for p in _split3(f_blk * LOG2E))
        gq = hi + pltpu.roll(mid, 1, axis=1) + pltpu.roll(lo, 2, axis=1) + jnp.where((slot >= 3) & (slot < 6), 1.0, 0.0)
        gk = jnp.where(slot < 3, 1.0, 0.0) - (pltpu.roll(hi, 3, axis=1) + pltpu.roll(mid, 4, axis=1)
                                              + pltpu.roll(lo, 5, axis=1))
        gk_ref[rows, :] = gk.astype(BF16)
        gqt_ref[:, rows] = gq.T.astype(BF16)


def _fcumsum(small, fb_vec, bsz, seqlen):
    return pl.pallas_call(
        _fcumsum_kernel,
        grid=(bsz,),
        in_specs=[
            pl.BlockSpec((seqlen, LANES), lambda b: (b, 0)),
            pl.BlockSpec((1, LANES), lambda b: (0, 0)),
        ],
        out_specs=[
            pl.BlockSpec((seqlen, LANES), lambda b: (b, 0)),
            pl.BlockSpec((LANES, seqlen), lambda b: (b, 0)),
        ],
        out_shape=[
            jax.ShapeDtypeStruct((bsz * seqlen, LANES), BF16),
            jax.ShapeDtypeStruct((bsz * LANES, seqlen), BF16),
        ],
        compiler_params=pltpu.CompilerParams(
            dimension_semantics=("arbitrary",), vmem_limit_bytes=VMEM_LIMIT),
        name="fcumsum",
    )(small, fb_vec)


def _attn_kernel(qt_ref, k_ref, gk_ref, vt_ref, gqt_ref, o_ref,
                 qaug_ref, s_a0, s_a1, s_b0, s_b1, acc_ref, m_ref, *, tq):
    hp = pl.program_id(1)
    i = pl.program_id(2)
    tk = KV_SUB
    sub_per_q = tq // tk
    s_refs = ((s_a0, s_a1), (s_b0, s_b1))
    ones_half = jnp.ones((HEAD_DIM, tk), BF16)

    def v_rows(h, sub):
        vt = vt_ref[sub]
        return (jnp.concatenate([vt[:HEAD_DIM], ones_half], axis=0) if h == 0
                else jnp.concatenate([ones_half, vt[HEAD_DIM:]], axis=0))


    grp = lax.broadcasted_iota(jnp.int32, (LANES, tq), 0) // 8
    gq_t = gqt_ref[...].astype(F32)
    zeros_half = jnp.zeros((HEAD_DIM, tq), BF16)
    qaug_ref[0, :HEAD_DIM] = qt_ref[:HEAD_DIM, :]
    qaug_ref[0, HEAD_DIM:LANES] = zeros_half
    qaug_ref[1, :HEAD_DIM] = zeros_half
    qaug_ref[1, HEAD_DIM:LANES] = qt_ref[HEAD_DIM:, :]
    for h in range(2):
        qaug_ref[h, LANES:] = jnp.where(grp == 2 * hp + h, gq_t, 0.0).astype(BF16)
    acc_ref[...] = jnp.zeros(acc_ref.shape, F32)
    m_ref[...] = jnp.full(m_ref.shape, -jnp.inf, F32)

    def qk(sub, n0, slot):
        rows = pl.ds(pl.multiple_of(sub * tk, tk), tk)
        k_sub = jnp.concatenate([k_ref[rows, :], gk_ref[rows, :]], axis=1)
        for h in range(2):
            s_refs[h][slot][:, n0:] = jnp.dot(k_sub, qaug_ref[h, :, n0:], preferred_element_type=F32)

    def update(sub, n0, slot, keep=None):
        for h in range(2):
            s = s_refs[h][slot][:, n0:]
            if keep is not None:
                diag = jnp.where(keep, s[:, :tk], -jnp.inf)
                s = diag if n0 + tk == tq else jnp.concatenate([diag, s[:, tk:]], axis=1)
            m = m_ref[h, 0:1, n0:]
            mn = jnp.maximum(m, jnp.max(s, axis=0, keepdims=True))
            p = jnp.exp2(s - mn).astype(BF16)
            acc_ref[h, :, n0:] = (acc_ref[h, :, n0:] * jnp.exp2(m - mn)
                                  + jnp.dot(v_rows(h, sub), p, preferred_element_type=F32))
            m_ref[h, 0:1, n0:] = mn

    first_diag = i * sub_per_q
    qk(0, 0, 0)

    def body(j, carry):
        for c in range(sub_per_q):
            sub = j * sub_per_q + c
            qk(sub + 1, 0, (c + 1) % 2)
            update(sub, 0, c % 2)
        return carry

    lax.fori_loop(0, i, body, 0)

    keep = lax.broadcasted_iota(jnp.int32, (tk, tk), 0) <= lax.broadcasted_iota(jnp.int32, (tk, tk), 1)
    for c in range(sub_per_q):
        n0 = c * tk
        if c + 1 < sub_per_q:
            qk(first_diag + c + 1, n0 + tk, (c + 1) % 2)
        update(first_diag + c, n0, c % 2, keep)

    inv_a = 1.0 / acc_ref[0, HEAD_DIM:HEAD_DIM + 1, :]
    inv_b = 1.0 / acc_ref[1, 0:1, :]
    o_t = jnp.concatenate([acc_ref[0, :HEAD_DIM, :] * inv_a, acc_ref[1, HEAD_DIM:, :] * inv_b], axis=0)
    o_ref[...] = o_t.T.astype(BF16)


def _attention(proj, q_t, v_t, gk, gq_t, bsz, seqlen, d_model, tq):
    assert (tq // KV_SUB) % 2 == 0, "score buffers alternate per key sub-tile"
    n_hp = d_model // LANES
    per_row = d_model // LANES
    nq = seqlen // tq
    n_sub = seqlen // KV_SUB
    score_buf = pltpu.VMEM((KV_SUB, tq), F32)
    return pl.pallas_call(
        functools.partial(_attn_kernel, tq=tq),
        grid=(bsz, n_hp, nq),
        in_specs=[
            pl.BlockSpec((LANES, tq), lambda b, h, i: (h, b * nq + i)),
            pl.BlockSpec((seqlen, LANES), lambda b, h, i: (b, COL_K * per_row + h)),
            pl.BlockSpec((seqlen, LANES), lambda b, h, i: (b, 0)),
            pl.BlockSpec((n_sub, LANES, KV_SUB), lambda b, h, i: (b, h, 0)),
            pl.BlockSpec((LANES, tq), lambda b, h, i: (b, i)),
        ],
        out_specs=pl.BlockSpec((tq, LANES), lambda b, h, i: (b * nq + i, h)),
        out_shape=jax.ShapeDtypeStruct((bsz * seqlen, d_model), BF16),
        scratch_shapes=[
            pltpu.VMEM((2, 2 * LANES, tq), BF16),
            score_buf, score_buf, score_buf, score_buf,
            pltpu.VMEM((2, LANES, tq), F32),
            pltpu.VMEM((2, 8, tq), F32),
        ],
        compiler_params=pltpu.CompilerParams(
            dimension_semantics=("arbitrary", "arbitrary", "arbitrary"), vmem_limit_bytes=VMEM_LIMIT),
        name="fox_attention",
    )(q_t, proj, gk, v_t, gq_t)


def _ssd_chunk(c, xs_ref, bc_ref, s_ref, cw_ref, cb_ref, dtb_ref, alog_ref, dskip_ref,
               expand_ref, shift_ref, y_ref, prev_ref, st_ref, n_heads):
    q = SSM_CHUNK
    d_inner = xs_ref.shape[1]
    gn = SSM_GROUPS * SSM_STATE
    heads_per_group = n_heads // SSM_GROUPS
    gw = heads_per_group * HEAD_DIM

    @pl.when(c == 0)
    def _():
        prev_ref[...] = jnp.zeros(prev_ref.shape, BF16)
        st_ref[...] = jnp.zeros(st_ref.shape, F32)

    cur = jnp.concatenate([xs_ref[...], bc_ref[...]], axis=1)
    both = jnp.concatenate([cur, prev_ref[...]], axis=0)
    prev_ref[...] = cur
    acts = []
    for n in range(cur.shape[1] // CONV_COLS):
        cols = slice(n * CONV_COLS, (n + 1) * CONV_COLS)
        shifted = jnp.dot(shift_ref[...], both[:, cols], preferred_element_type=F32)
        conv = cb_ref[:, cols] + cw_ref[SSM_CONV - 1:SSM_CONV, cols] * cur[:, cols].astype(F32)
        for jj in range(SSM_CONV - 1):
            k = SSM_CONV - 1 - jj
            conv = conv + cw_ref[jj:jj + 1, cols] * shifted[(k - 1) * q:k * q, :]
        acts.append(_silu(conv))
        yield
    act = jnp.concatenate(acts, axis=1)
    xs = act[:, :d_inner]
    b16 = act[:, d_inner:d_inner + gn].astype(BF16)
    c16 = act[:, d_inner + gn:].astype(BF16)

    lane = lax.broadcasted_iota(jnp.int32, (q, LANES), 1)
    head_lane = lane % 8 == DT_LANE
    dt = jnp.where(head_lane, _softplus(s_ref[...] + dtb_ref[...]), 0.0)
    da = dt * (-jnp.exp(alog_ref[...]))
    tri = _lower_tri(q)
    cs = _tri_cumsum(tri.astype(BF16), da)
    yield
    cs2 = cs * LOG2E
    key_t = (cs2 - jnp.log2(dt)).T
    last = cs[q - 1:q, :]
    w_end = dt * jnp.exp(last - cs)
    ecs = jnp.exp(cs)
    yield

    def expand(v):
        hi, mid, _ = _split3(jnp.where(head_lane, v, 0.0))
        both = hi.astype(F32) + pltpu.roll(mid.astype(F32), 1, axis=1)
        return jnp.dot(both.astype(BF16), expand_ref[...], preferred_element_type=F32)

    w_e = expand(w_end)
    ecs_e = expand(ecs)
    xw16 = (xs * w_e).astype(BF16)
    yield

    xs16 = xs.astype(BF16)
    first_half = lax.broadcasted_iota(jnp.int32, (q, LANES), 1) < HEAD_DIM

    y_parts = []
    for g in range(SSM_GROUPS):
        bg = b16[:, g * SSM_STATE:(g + 1) * SSM_STATE]
        cg = c16[:, g * SSM_STATE:(g + 1) * SSM_STATE]
        cb = lax.dot_general(cg, bg, (((1,), (1,)), ((), ())), preferred_element_type=F32)
        st_prev = st_ref[g]
        y_off = jnp.dot(cg, st_prev.astype(BF16), preferred_element_type=F32)
        yield
        for pair in range(heads_per_group // 2):
            lo_lane = g * gw + pair * LANES
            halves = []
            for half in range(2):
                h = 8 * (g * heads_per_group + pair * 2 + half) + DT_LANE
                seg = cs2[:, h:h + 1] - key_t[h:h + 1, :]
                mmat = (cb * jnp.exp2(jnp.where(tri, seg, -jnp.inf))).astype(BF16)
                halves.append(jnp.dot(mmat, xs16[:, lo_lane:lo_lane + LANES], preferred_element_type=F32))
            y_pair = jnp.where(first_half, halves[0], halves[1])
            y_pair = y_pair + y_off[:, pair * LANES:(pair + 1) * LANES] * ecs_e[:, lo_lane:lo_lane + LANES]
            y_parts.append(y_pair)
            yield
        bg_t = bg.astype(F32).T.astype(BF16)
        new_st = jnp.dot(bg_t, xw16[:, g * gw:(g + 1) * gw], preferred_element_type=F32)
        st_ref[g] = st_prev * ecs_e[q - 1:q, g * gw:(g + 1) * gw] + new_st
        yield

    y_ref[...] = (jnp.concatenate(y_parts, axis=1) + xs * dskip_ref[...]).astype(BF16)
    yield


def _ssd_kernel(xs_ref, bc_ref, s_ref, cw_ref, cb_ref, dtb_ref, alog_ref, dskip_ref,
                expand_ref, shift_ref, y_ref, prev_ref, st_ref, *, n_heads):
    c = pl.program_id(1)
    chunks = [_ssd_chunk(c, xs_ref.at[seq], bc_ref.at[seq], s_ref.at[seq], cw_ref, cb_ref, dtb_ref, alog_ref,
                         dskip_ref, expand_ref, shift_ref, y_ref.at[seq], prev_ref.at[seq], st_ref.at[seq], n_heads)
              for seq in range(xs_ref.shape[0])]
    for _ in zip(*chunks):
        pass


def _ssd(proj, small, conv_w, conv_b, dtb_vec, alog_vec, dskip_e, expand_m, shift_m, bsz, seqlen, d_model, n_heads):
    nc = seqlen // SSM_CHUNK
    q = SSM_CHUNK
    conv_dim = conv_w.shape[1]
    gw = (n_heads // SSM_GROUPS) * HEAD_DIM
    nb = 2 if bsz % 2 == 0 else 1
    proj3 = proj.reshape(bsz, seqlen, proj.shape[1])
    const = lambda b, c: (0, 0)
    y = pl.pallas_call(
        functools.partial(_ssd_kernel, n_heads=n_heads),
        grid=(bsz // nb, nc),
        in_specs=[
            pl.BlockSpec((nb, q, d_model), lambda b, c: (b, c, COL_XS)),
            pl.BlockSpec((nb, q, d_model), lambda b, c: (b, c, COL_BC)),
            pl.BlockSpec((nb, q, LANES), lambda b, c: (b, c, 0)),
            pl.BlockSpec((SSM_CONV, conv_dim), const),
            pl.BlockSpec((1, conv_dim), const),
            pl.BlockSpec((1, LANES), const),
            pl.BlockSpec((1, LANES), const),
            pl.BlockSpec((1, d_model), const),
            pl.BlockSpec((LANES, d_model), const),
            pl.BlockSpec(((SSM_CONV - 1) * q, 2 * q), const),
        ],
        out_specs=pl.BlockSpec((nb, q, d_model), lambda b, c: (b, c, 0)),
        out_shape=jax.ShapeDtypeStruct((bsz, seqlen, d_model), BF16),
        scratch_shapes=[
            pltpu.VMEM((nb, q, conv_dim), BF16),
            pltpu.VMEM((nb, SSM_GROUPS, SSM_STATE, gw), F32),
        ],
        compiler_params=pltpu.CompilerParams(
            dimension_semantics=("arbitrary", "arbitrary"), vmem_limit_bytes=VMEM_LIMIT),
        name="ssd_branch",
    )(proj3, proj3, small.reshape(bsz, seqlen, LANES), conv_w, conv_b, dtb_vec, alog_vec, dskip_e, expand_m, shift_m)
    return y.reshape(bsz * seqlen, d_model)


def _half_silu_gate(v16, zh16):
    return v16 * zh16 * (jnp.tanh(zh16) + 1.0)


def _merge_kernel(ys_ref, zs_ref, oa_ref, za_ref, gs_ref, ga_ref, x_ref, ws_ref, wa_ref, wo_ref, pg_ref, o_ref):
    y_att = _half_silu_gate(oa_ref[...], za_ref[...])
    p_att = jnp.dot(y_att, wa_ref[...], preferred_element_type=F32)
    yg = _half_silu_gate(ys_ref[...], zs_ref[...]).astype(F32)
    gw = yg.shape[1] // SSM_GROUPS
    normed = []
    for g in range(SSM_GROUPS):
        blk = yg[:, g * gw:(g + 1) * gw]
        normed.append(blk * lax.rsqrt(jnp.mean(blk * blk, axis=-1, keepdims=True) + RMS_EPS))
    y_ssm = jnp.concatenate(normed, axis=1).astype(BF16)
    p_ssm = jnp.dot(y_ssm, ws_ref[...], preferred_element_type=F32)
    gate_s = (jnp.tanh(gs_ref[...]) + 1.0).astype(F32)
    gate_a = (jnp.tanh(ga_ref[...]) + 1.0).astype(F32)
    merged = gate_s * p_ssm + gate_a * p_att
    out = jnp.dot(merged.astype(BF16), wo_ref[...], preferred_element_type=F32)
    ms = jnp.mean(out * out, axis=-1, keepdims=True)
    o_ref[...] = x_ref[...] + out * lax.rsqrt(ms + RMS_EPS) * pg_ref[...]


def _merge(y_scan, o_att, proj, x2, w_ssm, w_att, w_out, post_g, tm):
    m, d = x2.shape
    row = lambda i: (i, 0)
    full = lambda i: (0, 0)
    return pl.pallas_call(
        _merge_kernel,
        grid=(m // tm,),
        in_specs=[
            pl.BlockSpec((tm, d), row),
            pl.BlockSpec((tm, d), lambda i: (i, COL_Z_SSM)),
            pl.BlockSpec((tm, d), row),
            pl.BlockSpec((tm, d), lambda i: (i, COL_Z_ATT)),
            pl.BlockSpec((tm, d), lambda i: (i, COL_G_SSM)),
            pl.BlockSpec((tm, d), lambda i: (i, COL_G_ATT)),
            pl.BlockSpec((tm, d), row),
            pl.BlockSpec((d, d), full),
            pl.BlockSpec((d, d), full),
            pl.BlockSpec((d, d), full),
            pl.BlockSpec((1, d), full),
        ],
        out_specs=pl.BlockSpec((tm, d), row),
        out_shape=jax.ShapeDtypeStruct((m, d), F32),
        compiler_params=pltpu.CompilerParams(
            dimension_semantics=("arbitrary",), vmem_limit_bytes=VMEM_LIMIT),
        name="gated_merge",
    )(y_scan, proj, o_att, proj, proj, proj, x2, w_ssm, w_att, w_out, post_g)


def _head_lanes(v, slot):
    cols = [jnp.zeros_like(v, dtype=F32)] * 8
    cols[slot] = v.astype(F32)
    return jnp.stack(cols, axis=-1).reshape(1, LANES)


def _layer(x, pre_g, w_in, conv_w, conv_b, dt_bias, a_log, d_skip, ssm_norm_g, fgate_b,
           w_branch_ssm, w_branch_att, w_out, post_g):
    bsz, seqlen, d = x.shape
    n_heads = d // HEAD_DIM
    assert n_heads * 8 == LANES, "small-projection lane layout assumes 16 heads"
    gn = SSM_GROUPS * SSM_STATE
    sizes = (d, d + 2 * gn, n_heads, d, d, d, n_heads, d, d, d)
    offs = [0]
    for sz in sizes:
        offs.append(offs[-1] + sz)
    z_ssm, xbc, dt_w, q_w, k_w, v_w, f_w, z_att, g_ssm, g_att = (w_in[:, offs[n]:offs[n + 1]] for n in range(10))
    w_big = jnp.concatenate([0.5 * z_ssm, xbc, k_w, 0.5 * z_att, 0.5 * g_ssm, 0.5 * g_att], axis=1).astype(BF16)
    w_t = jnp.concatenate([(q_w * (HEAD_DIM ** -0.5 * LOG2E)).T, v_w.T], axis=0).astype(BF16)
    slots = [jnp.zeros_like(f_w)] * 8
    slots[F_LANE], slots[DT_LANE] = f_w, dt_w
    w_small = jnp.stack(slots, axis=-1).reshape(d, LANES).astype(BF16)

    x2 = x.reshape(bsz * seqlen, d)
    proj, small, q_t, v_t = _in_proj(x2, pre_g.reshape(1, d), w_big, w_t, w_small, tm=min(1024, bsz * seqlen))

    gk, gq_t = _fcumsum(small, _head_lanes(fgate_b, F_LANE), bsz, seqlen)
    o_att = _attention(proj, q_t, v_t, gk, gq_t, bsz, seqlen, d, tq=min(2048, seqlen))

    head_of_col = np.arange(d) // HEAD_DIM
    lane_ids = np.arange(LANES)[:, None]
    expand_m = jnp.asarray((lane_ids == 8 * head_of_col[None, :] + DT_LANE)
                           | (lane_ids == 8 * head_of_col[None, :] + DT_LANE + 1), BF16)
    t_ids = np.arange(SSM_CHUNK)[:, None]
    src = np.arange(2 * SSM_CHUNK)[None, :]
    shift_m = jnp.asarray(np.concatenate([src == np.where(t_ids >= k, t_ids - k, 2 * SSM_CHUNK + t_ids - k)
                                          for k in range(1, SSM_CONV)], axis=0), BF16)
    y_scan = _ssd(proj, small, conv_w, conv_b.reshape(1, -1), _head_lanes(dt_bias, DT_LANE), _head_lanes(a_log, DT_LANE),
                 jnp.repeat(d_skip.astype(F32), HEAD_DIM).reshape(1, d), expand_m, shift_m,
                 bsz, seqlen, d, n_heads)

    out = _merge(y_scan, o_att, proj, x2, (ssm_norm_g.astype(F32)[:, None] * w_branch_ssm).astype(BF16),
                 w_branch_att.astype(BF16), (0.5 * w_out).astype(BF16), post_g.reshape(1, d), tm=min(512, bsz * seqlen))
    return out.reshape(bsz, seqlen, d)


def kernel(x, pre_norm_g, w_in, conv_w, conv_b, dt_bias, a_log, d_skip, ssm_norm_g, fgate_b,
           w_branch_ssm, w_branch_att, w_out, post_norm_g):
    for i in range(pre_norm_g.shape[0]):
        x = _layer(x, pre_norm_g[i], w_in[i], conv_w[i], conv_b[i], dt_bias[i], a_log[i], d_skip[i],
                   ssm_norm_g[i], fgate_b[i], w_branch_ssm[i], w_branch_att[i], w_out[i], post_norm_g[i])
    return x
```

```python
import functools

import jax
import jax.numpy as jnp
import numpy as np
from jax import lax
from jax.experimental import pallas as pl
from jax.experimental.pallas import tpu as pltpu

F32 = jnp.float32
BF16 = jnp.bfloat16

RMS_EPS = 1e-6
HEAD_DIM = 64
SSM_GROUPS = 4
SSM_STATE = 128
SSM_CONV = 4
SSM_CHUNK = 128
CONV_COLS = 512
NORM_ROWS = 256
KV_SUB = 256
LANES = 128
VMEM_LIMIT = 48 * 1024 * 1024

COL_Z_SSM, COL_XS, COL_BC, COL_K, COL_Z_ATT, COL_G_SSM, COL_G_ATT = range(7)
N_COL_BLOCKS = 7
PROJ_STEPS = 4
N_T_BLOCKS = 2
LOG2E = 1.4426950408889634
F_LANE = 0
DT_LANE = 4


def _sigmoid(x):
    return 0.5 * jnp.tanh(0.5 * x) + 0.5


def _silu(x):
    return x * _sigmoid(x)


def _softplus(x):
    return jnp.maximum(x, 0.0) + jnp.log(1.0 + jnp.exp(-jnp.abs(x)))


def _split3(x):
    hi = x.astype(BF16)
    r1 = x - hi.astype(F32)
    mid = r1.astype(BF16)
    lo = (r1 - mid.astype(F32)).astype(BF16)
    return hi, mid, lo


def _tri_cumsum(tri_bf16, x):
    hi, mid, lo = _split3(x)
    dot = functools.partial(jnp.dot, preferred_element_type=F32)
    return dot(tri_bf16, hi) + dot(tri_bf16, mid) + dot(tri_bf16, lo)


def _lower_tri(n):
    r = lax.broadcasted_iota(jnp.int32, (n, n), 0)
    c = lax.broadcasted_iota(jnp.int32, (n, n), 1)
    return r >= c


def _in_proj_kernel(x_ref, g_ref, w_ref, wt_ref, ws_ref, p_ref, s_ref, qt_ref, vta_ref, vtb_ref, h_ref):
    j = pl.program_id(1)
    nt = (((1,), (1,)), ((), ()))

    @pl.when(j == 0)
    def _():
        for r0 in range(0, x_ref.shape[0], NORM_ROWS):
            rows = slice(r0, r0 + NORM_ROWS)
            x = x_ref[rows, :]
            ms = jnp.mean(x * x, axis=-1, keepdims=True)
            h = (x * lax.rsqrt(ms + RMS_EPS) * g_ref[...]).astype(BF16)
            h_ref[rows, :] = h
            s_ref[rows, :] = jnp.dot(h, ws_ref[...], preferred_element_type=F32)
            p_ref[rows, :] = jnp.dot(h, w_ref[...], preferred_element_type=F32).astype(BF16)

    @pl.when((j > 0) & (j < PROJ_STEPS))
    def _():
        p_ref[...] = jnp.dot(h_ref[...], w_ref[...], preferred_element_type=F32).astype(BF16)

    @pl.when(j == PROJ_STEPS)
    def _():
        d = qt_ref.shape[0]
        qt_ref[...] = lax.dot_general(wt_ref[:d, :], h_ref[...], nt, preferred_element_type=F32).astype(BF16)
        vt = lax.dot_general(wt_ref[d:, :], h_ref[...], nt, preferred_element_type=F32)
        first = (lax.broadcasted_iota(jnp.int32, vt.shape, 0) % LANES) < HEAD_DIM
        vta = jnp.where(first, vt, 1.0).astype(BF16)
        vtb = jnp.where(first, 1.0, vt).astype(BF16)
        for c in range(vta_ref.shape[0]):
            vta_ref[c] = vta[:, c * KV_SUB:(c + 1) * KV_SUB]
            vtb_ref[c] = vtb[:, c * KV_SUB:(c + 1) * KV_SUB]


def _in_proj(x2, g, w_big, w_t, w_small, tm):
    m, d = x2.shape
    last = PROJ_STEPS - 1
    wide = N_COL_BLOCKS * d // PROJ_STEPS
    assert wide % LANES == 0 and wide * PROJ_STEPS == N_COL_BLOCKS * d
    sub = tm // KV_SUB
    vt_shape = jax.ShapeDtypeStruct((m // KV_SUB, d, KV_SUB), BF16)
    return pl.pallas_call(
        _in_proj_kernel,
        grid=(m // tm, PROJ_STEPS + 1),
        in_specs=[
            pl.BlockSpec((tm, d), lambda i, j: (i, 0)),
            pl.BlockSpec((1, d), lambda i, j: (0, 0)),
            pl.BlockSpec((d, wide), lambda i, j: (0, jnp.minimum(j, last))),
            pl.BlockSpec((N_T_BLOCKS * d, d), lambda i, j: (0, 0)),
            pl.BlockSpec((d, LANES), lambda i, j: (0, 0)),
        ],
        out_specs=[
            pl.BlockSpec((tm, wide), lambda i, j: (i, jnp.minimum(j, last))),
            pl.BlockSpec((tm, LANES), lambda i, j: (i, 0)),
            pl.BlockSpec((d, tm), lambda i, j: (0, i)),
            pl.BlockSpec((sub, d, KV_SUB), lambda i, j: (i, 0, 0)),
            pl.BlockSpec((sub, d, KV_SUB), lambda i, j: (i, 0, 0)),
        ],
        out_shape=[
            jax.ShapeDtypeStruct((m, N_COL_BLOCKS * d), BF16),
            jax.ShapeDtypeStruct((m, LANES), F32),
            jax.ShapeDtypeStruct((d, m), BF16),
            vt_shape,
            vt_shape,
        ],
        scratch_shapes=[pltpu.VMEM((tm, d), BF16)],
        compiler_params=pltpu.CompilerParams(
            dimension_semantics=("arbitrary", "arbitrary"), vmem_limit_bytes=VMEM_LIMIT),
        name="in_proj",
    )(x2, g, w_big, w_t, w_small)


def _fcumsum_kernel(s_ref, b_ref, gk_ref, gqt_ref):
    seqlen = s_ref.shape[0]
    blk = SSM_CHUNK
    tri = _lower_tri(blk).astype(BF16)
    slot = lax.broadcasted_iota(jnp.int32, (blk, LANES), 1) % 8
    carry = jnp.zeros((1, LANES), F32)
    for c in range(seqlen // blk):
        rows = slice(c * blk, (c + 1) * blk)
        raw = s_ref[rows, :] + b_ref[...]
        logf = jnp.where(slot == F_LANE, -_softplus(-raw), 0.0)
        f_blk = _tri_cumsum(tri, logf) + carry
        carry = f_blk[blk - 1:blk, :]
        hi, mid, lo = (p.astype(F32) for p in _split3(f_blk * LOG2E))
        gq = hi + pltpu.roll(mid, 1, axis=1) + pltpu.roll(lo, 2, axis=1) + jnp.where((slot >= 3) & (slot < 6), 1.0, 0.0)
        gk = jnp.where(slot < 3, 1.0, 0.0) - (pltpu.roll(hi, 3, axis=1) + pltpu.roll(mid, 4, axis=1)
                                              + pltpu.roll(lo, 5, axis=1))
        gk_ref[rows, :] = gk.astype(BF16)
        gqt_ref[:, rows] = gq.T.astype(BF16)


def _fcumsum(small, fb_vec, bsz, seqlen):
    return pl.pallas_call(
        _fcumsum_kernel,
        grid=(bsz,),
        in_specs=[
            pl.BlockSpec((seqlen, LANES), lambda b: (b, 0)),
            pl.BlockSpec((1, LANES), lambda b: (0, 0)),
        ],
        out_specs=[
            pl.BlockSpec((seqlen, LANES), lambda b: (b, 0)),
            pl.BlockSpec((LANES, seqlen), lambda b: (b, 0)),
        ],
        out_shape=[
            jax.ShapeDtypeStruct((bsz * seqlen, LANES), BF16),
            jax.ShapeDtypeStruct((bsz * LANES, seqlen), BF16),
        ],
        compiler_params=pltpu.CompilerParams(
            dimension_semantics=("arbitrary",), vmem_limit_bytes=VMEM_LIMIT),
        name="fcumsum",
    )(small, fb_vec)


def _attn_kernel(qt_ref, k_ref, gk_ref, vta_ref, vtb_ref, gqt_ref, o_ref,
                 qaug_ref, s_a0, s_a1, s_b0, s_b1, acc_ref, m_ref, *, tq):
    hp = pl.program_id(1)
    i = pl.program_id(2)
    tk = KV_SUB
    sub_per_q = tq // tk
    s_refs = ((s_a0, s_a1), (s_b0, s_b1))
    vt_refs = (vta_ref, vtb_ref)

    grp = lax.broadcasted_iota(jnp.int32, (LANES, tq), 0) // 8
    gq_t = gqt_ref[...].astype(F32)
    zeros_half = jnp.zeros((HEAD_DIM, tq), BF16)
    qaug_ref[0, :HEAD_DIM] = qt_ref[:HEAD_DIM, :]
    qaug_ref[0, HEAD_DIM:LANES] = zeros_half
    qaug_ref[1, :HEAD_DIM] = zeros_half
    qaug_ref[1, HEAD_DIM:LANES] = qt_ref[HEAD_DIM:, :]
    for h in range(2):
        qaug_ref[h, LANES:] = jnp.where(grp == 2 * hp + h, gq_t, 0.0).astype(BF16)
    acc_ref[...] = jnp.zeros(acc_ref.shape, F32)
    m_ref[...] = jnp.full(m_ref.shape, -jnp.inf, F32)

    def qk(sub, n0, slot):
        rows = pl.ds(pl.multiple_of(sub * tk, tk), tk)
        k_sub = jnp.concatenate([k_ref[rows, :], gk_ref[rows, :]], axis=1)
        for h in range(2):
            s_refs[h][slot][:, n0:] = jnp.dot(k_sub, qaug_ref[h, :, n0:], preferred_element_type=F32)

    def update(sub, n0, slot, keep=None):
        for h in range(2):
            s = s_refs[h][slot][:, n0:]
            if keep is not None:
                diag = jnp.where(keep, s[:, :tk], -jnp.inf)
                s = diag if n0 + tk == tq else jnp.concatenate([diag, s[:, tk:]], axis=1)
            m = m_ref[h, 0:1, n0:]
            mn = jnp.maximum(m, jnp.max(s, axis=0, keepdims=True))
            p = jnp.exp2(s - mn).astype(BF16)
            acc_ref[h, :, n0:] = (acc_ref[h, :, n0:] * jnp.exp2(m - mn)
                                  + jnp.dot(vt_refs[h][sub], p, preferred_element_type=F32))
            m_ref[h, 0:1, n0:] = mn

    first_diag = i * sub_per_q
    qk(0, 0, 0)

    def body(j, carry):
        for c in range(sub_per_q):
            sub = j * sub_per_q + c
            qk(sub + 1, 0, (c + 1) % 2)
            update(sub, 0, c % 2)
        return carry

    lax.fori_loop(0, i, body, 0)

    keep = lax.broadcasted_iota(jnp.int32, (tk, tk), 0) <= lax.broadcasted_iota(jnp.int32, (tk, tk), 1)
    for c in range(sub_per_q):
        n0 = c * tk
        if c + 1 < sub_per_q:
            qk(first_diag + c + 1, n0 + tk, (c + 1) % 2)
        update(first_diag + c, n0, c % 2, keep)

    inv_a = 1.0 / acc_ref[0, HEAD_DIM:HEAD_DIM + 1, :]
    inv_b = 1.0 / acc_ref[1, 0:1, :]
    o_t = jnp.concatenate([acc_ref[0, :HEAD_DIM, :] * inv_a, acc_ref[1, HEAD_DIM:, :] * inv_b], axis=0)
    o_ref[...] = o_t.T.astype(BF16)


def _attention(proj, q_t, vta, vtb, gk, gq_t, bsz, seqlen, d_model, tq):
    assert (tq // KV_SUB) % 2 == 0, "score buffers alternate per key sub-tile"
    n_hp = d_model // LANES
    per_row = d_model // LANES
    nq = seqlen // tq
    n_sub = seqlen // KV_SUB
    score_buf = pltpu.VMEM((KV_SUB, tq), F32)
    return pl.pallas_call(
        functools.partial(_attn_kernel, tq=tq),
        grid=(bsz, n_hp, nq),
        in_specs=[
            pl.BlockSpec((LANES, tq), lambda b, h, i: (h, b * nq + i)),
            pl.BlockSpec((seqlen, LANES), lambda b, h, i: (b, COL_K * per_row + h)),
            pl.BlockSpec((seqlen, LANES), lambda b, h, i: (b, 0)),
            pl.BlockSpec((n_sub, LANES, KV_SUB), lambda b, h, i: (b, h, 0)),
            pl.BlockSpec((n_sub, LANES, KV_SUB), lambda b, h, i: (b, h, 0)),
            pl.BlockSpec((LANES, tq), lambda b, h, i: (b, i)),
        ],
        out_specs=pl.BlockSpec((tq, LANES), lambda b, h, i: (b * nq + i, h)),
        out_shape=jax.ShapeDtypeStruct((bsz * seqlen, d_model), BF16),
        scratch_shapes=[
            pltpu.VMEM((2, 2 * LANES, tq), BF16),
            score_buf, score_buf, score_buf, score_buf,
            pltpu.VMEM((2, LANES, tq), F32),
            pltpu.VMEM((2, 8, tq), F32),
        ],
        compiler_params=pltpu.CompilerParams(
            dimension_semantics=("arbitrary", "arbitrary", "arbitrary"), vmem_limit_bytes=VMEM_LIMIT),
        name="fox_attention",
    )(q_t, proj, gk, vta, vtb, gq_t)


def _ssd_chunk(c, xs_ref, bc_ref, s_ref, cw_ref, cb_ref, dtb_ref, alog_ref, dskip_ref,
               expand_ref, shift_ref, y_ref, prev_ref, st_ref, n_heads):
    q = SSM_CHUNK
    d_inner = xs_ref.shape[1]
    gn = SSM_GROUPS * SSM_STATE
    heads_per_group = n_heads // SSM_GROUPS
    gw = heads_per_group * HEAD_DIM

    @pl.when(c == 0)
    def _():
        prev_ref[...] = jnp.zeros(prev_ref.shape, BF16)
        st_ref[...] = jnp.zeros(st_ref.shape, F32)

    cur = jnp.concatenate([xs_ref[...], bc_ref[...]], axis=1)
    both = jnp.concatenate([cur, prev_ref[...]], axis=0)
    prev_ref[...] = cur
    acts = []
    for n in range(cur.shape[1] // CONV_COLS):
        cols = slice(n * CONV_COLS, (n + 1) * CONV_COLS)
        shifted = jnp.dot(shift_ref[...], both[:, cols], preferred_element_type=F32)
        conv = cb_ref[:, cols] + cw_ref[SSM_CONV - 1:SSM_CONV, cols] * cur[:, cols].astype(F32)
        for jj in range(SSM_CONV - 1):
            k = SSM_CONV - 1 - jj
            conv = conv + cw_ref[jj:jj + 1, cols] * shifted[(k - 1) * q:k * q, :]
        acts.append(_silu(conv))
        yield
    act = jnp.concatenate(acts, axis=1)
    xs = act[:, :d_inner]
    b16 = act[:, d_inner:d_inner + gn].astype(BF16)
    c16 = act[:, d_inner + gn:].astype(BF16)

    lane = lax.broadcasted_iota(jnp.int32, (q, LANES), 1)
    head_lane = lane % 8 == DT_LANE
    dt = jnp.where(head_lane, _softplus(s_ref[...] + dtb_ref[...]), 0.0)
    da = dt * (-jnp.exp(alog_ref[...]))
    tri = _lower_tri(q)
    cs = _tri_cumsum(tri.astype(BF16), da)
    yield
    cs2 = cs * LOG2E
    key_t = (cs2 - jnp.log2(dt)).T
    last = cs[q - 1:q, :]
    w_end = dt * jnp.exp(last - cs)
    ecs = jnp.exp(cs)
    yield

    def expand(v):
        hi, mid, _ = _split3(jnp.where(head_lane, v, 0.0))
        both = hi.astype(F32) + pltpu.roll(mid.astype(F32), 1, axis=1)
        return jnp.dot(both.astype(BF16), expand_ref[...], preferred_element_type=F32)

    w_e = expand(w_end)
    ecs_e = expand(ecs)
    xw16 = (xs * w_e).astype(BF16)
    yield

    xs16 = xs.astype(BF16)
    first_half = lax.broadcasted_iota(jnp.int32, (q, LANES), 1) < HEAD_DIM

    y_parts = []
    for g in range(SSM_GROUPS):
        bg = b16[:, g * SSM_STATE:(g + 1) * SSM_STATE]
        cg = c16[:, g * SSM_STATE:(g + 1) * SSM_STATE]
        cb = lax.dot_general(cg, bg, (((1,), (1,)), ((), ())), preferred_element_type=F32)
        st_prev = st_ref[g]
        y_off = jnp.dot(cg, st_prev.astype(BF16), preferred_element_type=F32)
        yield
        for pair in range(heads_per_group // 2):
            lo_lane = g * gw + pair * LANES
            halves = []
            for half in range(2):
                h = 8 * (g * heads_per_group + pair * 2 + half) + DT_LANE
                seg = cs2[:, h:h + 1] - key_t[h:h + 1, :]
                mmat = (cb * jnp.exp2(jnp.where(tri, seg, -jnp.inf))).astype(BF16)
                halves.append(jnp.dot(mmat, xs16[:, lo_lane:lo_lane + LANES], preferred_element_type=F32))
            y_pair = jnp.where(first_half, halves[0], halves[1])
            y_pair = y_pair + y_off[:, pair * LANES:(pair + 1) * LANES] * ecs_e[:, lo_lane:lo_lane + LANES]
            y_parts.append(y_pair)
            yield
        bg_t = bg.astype(F32).T.astype(BF16)
        new_st = jnp.dot(bg_t, xw16[:, g * gw:(g + 1) * gw], preferred_element_type=F32)
        st_ref[g] = st_prev * ecs_e[q - 1:q, g * gw:(g + 1) * gw] + new_st
        yield

    y_ref[...] = (jnp.concatenate(y_parts, axis=1) + xs * dskip_ref[...]).astype(BF16)
    yield


def _ssd_kernel(xs_ref, bc_ref, s_ref, cw_ref, cb_ref, dtb_ref, alog_ref, dskip_ref,
                expand_ref, shift_ref, y_ref, prev_ref, st_ref, *, n_heads):
    c = pl.program_id(1)
    chunks = [_ssd_chunk(c, xs_ref.at[seq], bc_ref.at[seq], s_ref.at[seq], cw_ref, cb_ref, dtb_ref, alog_ref,
                         dskip_ref, expand_ref, shift_ref, y_ref.at[seq], prev_ref.at[seq], st_ref.at[seq], n_heads)
              for seq in range(xs_ref.shape[0])]
    for _ in zip(*chunks):
        pass


def _ssd(proj, small, conv_w, conv_b, dtb_vec, alog_vec, dskip_e, expand_m, shift_m, bsz, seqlen, d_model, n_heads):
    nc = seqlen // SSM_CHUNK
    q = SSM_CHUNK
    conv_dim = conv_w.shape[1]
    gw = (n_heads // SSM_GROUPS) * HEAD_DIM
    nb = 2 if bsz % 2 == 0 else 1
    proj3 = proj.reshape(bsz, seqlen, proj.shape[1])
    const = lambda b, c: (0, 0)
    y = pl.pallas_call(
        functools.partial(_ssd_kernel, n_heads=n_heads),
        grid=(bsz // nb, nc),
        in_specs=[
            pl.BlockSpec((nb, q, d_model), lambda b, c: (b, c, COL_XS)),
            pl.BlockSpec((nb, q, d_model), lambda b, c: (b, c, COL_BC)),
            pl.BlockSpec((nb, q, LANES), lambda b, c: (b, c, 0)),
            pl.BlockSpec((SSM_CONV, conv_dim), const),
            pl.BlockSpec((1, conv_dim), const),
            pl.BlockSpec((1, LANES), const),
            pl.BlockSpec((1, LANES), const),
            pl.BlockSpec((1, d_model), const),
            pl.BlockSpec((LANES, d_model), const),
            pl.BlockSpec(((SSM_CONV - 1) * q, 2 * q), const),
        ],
        out_specs=pl.BlockSpec((nb, q, d_model), lambda b, c: (b, c, 0)),
        out_shape=jax.ShapeDtypeStruct((bsz, seqlen, d_model), BF16),
        scratch_shapes=[
            pltpu.VMEM((nb, q, conv_dim), BF16),
            pltpu.VMEM((nb, SSM_GROUPS, SSM_STATE, gw), F32),
        ],
        compiler_params=pltpu.CompilerParams(
            dimension_semantics=("arbitrary", "arbitrary"), vmem_limit_bytes=VMEM_LIMIT),
        name="ssd_branch",
    )(proj3, proj3, small.reshape(bsz, seqlen, LANES), conv_w, conv_b, dtb_vec, alog_vec, dskip_e, expand_m, shift_m)
    return y.reshape(bsz * seqlen, d_model)


def _half_silu_gate(v16, zh16):
    return v16 * zh16 * (jnp.tanh(zh16) + 1.0)


def _merge_kernel(ys_ref, zs_ref, oa_ref, za_ref, gs_ref, ga_ref, x_ref, ws_ref, wa_ref, wo_ref, pg_ref, o_ref):
    y_att = _half_silu_gate(oa_ref[...], za_ref[...])
    p_att = jnp.dot(y_att, wa_ref[...], preferred_element_type=F32)
    yg = _half_silu_gate(ys_ref[...], zs_ref[...]).astype(F32)
    gw = yg.shape[1] // SSM_GROUPS
    normed = []
    for g in range(SSM_GROUPS):
        blk = yg[:, g * gw:(g + 1) * gw]
        normed.append(blk * lax.rsqrt(jnp.mean(blk * blk, axis=-1, keepdims=True) + RMS_EPS))
    y_ssm = jnp.concatenate(normed, axis=1).astype(BF16)
    p_ssm = jnp.dot(y_ssm, ws_ref[...], preferred_element_type=F32)
    gate_s = (jnp.tanh(gs_ref[...]) + 1.0).astype(F32)
    gate_a = (jnp.tanh(ga_ref[...]) + 1.0).astype(F32)
    merged = gate_s * p_ssm + gate_a * p_att
    out = jnp.dot(merged.astype(BF16), wo_ref[...], preferred_element_type=F32)
    ms = jnp.mean(out * out, axis=-1, keepdims=True)
    o_ref[...] = x_ref[...] + out * lax.rsqrt(ms + RMS_EPS) * pg_ref[...]


def _merge(y_scan, o_att, proj, x2, w_ssm, w_att, w_out, post_g, tm):
    m, d = x2.shape
    row = lambda i: (i, 0)
    full = lambda i: (0, 0)
    return pl.pallas_call(
        _merge_kernel,
        grid=(m // tm,),
        in_specs=[
            pl.BlockSpec((tm, d), row),
            pl.BlockSpec((tm, d), lambda i: (i, COL_Z_SSM)),
            pl.BlockSpec((tm, d), row),
            pl.BlockSpec((tm, d), lambda i: (i, COL_Z_ATT)),
            pl.BlockSpec((tm, d), lambda i: (i, COL_G_SSM)),
            pl.BlockSpec((tm, d), lambda i: (i, COL_G_ATT)),
            pl.BlockSpec((tm, d), row),
            pl.BlockSpec((d, d), full),
            pl.BlockSpec((d, d), full),
            pl.BlockSpec((d, d), full),
            pl.BlockSpec((1, d), full),
        ],
        out_specs=pl.BlockSpec((tm, d), row),
        out_shape=jax.ShapeDtypeStruct((m, d), F32),
        compiler_params=pltpu.CompilerParams(
            dimension_semantics=("arbitrary",), vmem_limit_bytes=VMEM_LIMIT),
        name="gated_merge",
    )(y_scan, proj, o_att, proj, proj, proj, x2, w_ssm, w_att, w_out, post_g)


def _head_lanes(v, slot):
    cols = [jnp.zeros_like(v, dtype=F32)] * 8
    cols[slot] = v.astype(F32)
    return jnp.stack(cols, axis=-1).reshape(1, LANES)


def _layer(x, pre_g, w_in, conv_w, conv_b, dt_bias, a_log, d_skip, ssm_norm_g, fgate_b,
           w_branch_ssm, w_branch_att, w_out, post_g):
    bsz, seqlen, d = x.shape
    n_heads = d // HEAD_DIM
    assert n_heads * 8 == LANES, "small-projection lane layout assumes 16 heads"
    gn = SSM_GROUPS * SSM_STATE
    sizes = (d, d + 2 * gn, n_heads, d, d, d, n_heads, d, d, d)
    offs = [0]
    for sz in sizes:
        offs.append(offs[-1] + sz)
    z_ssm, xbc, dt_w, q_w, k_w, v_w, f_w, z_att, g_ssm, g_att = (w_in[:, offs[n]:offs[n + 1]] for n in range(10))
    w_big = jnp.concatenate([0.5 * z_ssm, xbc, k_w, 0.5 * z_att, 0.5 * g_ssm, 0.5 * g_att], axis=1).astype(BF16)
    w_t = jnp.concatenate([(q_w * (HEAD_DIM ** -0.5 * LOG2E)).T, v_w.T], axis=0).astype(BF16)
    slots = [jnp.zeros_like(f_w)] * 8
    slots[F_LANE], slots[DT_LANE] = f_w, dt_w
    w_small = jnp.stack(slots, axis=-1).reshape(d, LANES).astype(BF16)

    x2 = x.reshape(bsz * seqlen, d)
    proj, small, q_t, vta, vtb = _in_proj(x2, pre_g.reshape(1, d), w_big, w_t, w_small, tm=min(1024, bsz * seqlen))

    gk, gq_t = _fcumsum(small, _head_lanes(fgate_b, F_LANE), bsz, seqlen)
    o_att = _attention(proj, q_t, vta, vtb, gk, gq_t, bsz, seqlen, d, tq=min(2048, seqlen))

    head_of_col = np.arange(d) // HEAD_DIM
    lane_ids = np.arange(LANES)[:, None]
    expand_m = jnp.asarray((lane_ids == 8 * head_of_col[None, :] + DT_LANE)
                           | (lane_ids == 8 * head_of_col[None, :] + DT_LANE + 1), BF16)
    t_ids = np.arange(SSM_CHUNK)[:, None]
    src = np.arange(2 * SSM_CHUNK)[None, :]
    shift_m = jnp.asarray(np.concatenate([src == np.where(t_ids >= k, t_ids - k, 2 * SSM_CHUNK + t_ids - k)
                                          for k in range(1, SSM_CONV)], axis=0), BF16)
    y_scan = _ssd(proj, small, conv_w, conv_b.reshape(1, -1), _head_lanes(dt_bias, DT_LANE), _head_lanes(a_log, DT_LANE),
                 jnp.repeat(d_skip.astype(F32), HEAD_DIM).reshape(1, d), expand_m, shift_m,
                 bsz, seqlen, d, n_heads)

    out = _merge(y_scan, o_att, proj, x2, (ssm_norm_g.astype(F32)[:, None] * w_branch_ssm).astype(BF16),
                 w_branch_att.astype(BF16), (0.5 * w_out).astype(BF16), post_g.reshape(1, d), tm=min(512, bsz * seqlen))
    return out.reshape(bsz, seqlen, d)


def kernel(x, pre_norm_g, w_in, conv_w, conv_b, dt_bias, a_log, d_skip, ssm_norm_g, fgate_b,
           w_branch_ssm, w_branch_att, w_out, post_norm_g):
    for i in range(pre_norm_g.shape[0]):
        x = _layer(x, pre_norm_g[i], w_in[i], conv_w[i], conv_b[i], dt_bias[i], a_log[i], d_skip[i],
                   ssm_norm_g[i], fgate_b[i], w_branch_ssm[i], w_branch_att[i], w_out[i], post_norm_g[i])
    return x
```

```python
import functools

import jax
import jax.numpy as jnp
import numpy as np
from jax import lax
from jax.experimental import pallas as pl
from jax.experimental.pallas import tpu as pltpu

F32 = jnp.float32
BF16 = jnp.bfloat16

RMS_EPS = 1e-6
HEAD_DIM = 64
SSM_GROUPS = 4
SSM_STATE = 128
SSM_CONV = 4
SSM_CHUNK = 128
CONV_COLS = 512
SSD_CHUNKS_PER_STEP = 4
NORM_ROWS = 256
KV_SUB = 256
LANES = 128
VMEM_LIMIT = 48 * 1024 * 1024

COL_Z_SSM, COL_XS, COL_BC, COL_K, COL_Z_ATT, COL_G_SSM, COL_G_ATT = range(7)
N_COL_BLOCKS = 7
PROJ_STEPS = 4
N_T_BLOCKS = 2
LOG2E = 1.4426950408889634
F_LANE = 0
DT_LANE = 4


def _sigmoid(x):
    return 0.5 * jnp.tanh(0.5 * x) + 0.5


def _silu(x):
    return x * _sigmoid(x)


def _softplus(x):
    return jnp.maximum(x, 0.0) + jnp.log(1.0 + jnp.exp(-jnp.abs(x)))


def _split3(x):
    hi = x.astype(BF16)
    r1 = x - hi.astype(F32)
    mid = r1.astype(BF16)
    lo = (r1 - mid.astype(F32)).astype(BF16)
    return hi, mid, lo


def _tri_cumsum(tri_bf16, x):
    hi, mid, lo = _split3(x)
    dot = functools.partial(jnp.dot, preferred_element_type=F32)
    return dot(tri_bf16, hi) + dot(tri_bf16, mid) + dot(tri_bf16, lo)


def _lower_tri(n):
    r = lax.broadcasted_iota(jnp.int32, (n, n), 0)
    c = lax.broadcasted_iota(jnp.int32, (n, n), 1)
    return r >= c


def _in_proj_kernel(x_ref, g_ref, w_ref, wt_ref, ws_ref, p_ref, s_ref, qt_ref, vta_ref, vtb_ref, h_ref):
    j = pl.program_id(1)
    nt = (((1,), (1,)), ((), ()))

    @pl.when(j == 0)
    def _():
        for r0 in range(0, x_ref.shape[0], NORM_ROWS):
            rows = slice(r0, r0 + NORM_ROWS)
            x = x_ref[rows, :]
            ms = jnp.mean(x * x, axis=-1, keepdims=True)
            h = (x * lax.rsqrt(ms + RMS_EPS) * g_ref[...]).astype(BF16)
            h_ref[rows, :] = h
            s_ref[rows, :] = jnp.dot(h, ws_ref[...], preferred_element_type=F32)
            p_ref[rows, :] = jnp.dot(h, w_ref[...], preferred_element_type=F32).astype(BF16)

    @pl.when((j > 0) & (j < PROJ_STEPS))
    def _():
        p_ref[...] = jnp.dot(h_ref[...], w_ref[...], preferred_element_type=F32).astype(BF16)

    @pl.when(j == PROJ_STEPS)
    def _():
        d = qt_ref.shape[0]
        qt_ref[...] = lax.dot_general(wt_ref[:d, :], h_ref[...], nt, preferred_element_type=F32).astype(BF16)
        vt = lax.dot_general(wt_ref[d:, :], h_ref[...], nt, preferred_element_type=F32)
        first = (lax.broadcasted_iota(jnp.int32, vt.shape, 0) % LANES) < HEAD_DIM
        vta = jnp.where(first, vt, 1.0).astype(BF16)
        vtb = jnp.where(first, 1.0, vt).astype(BF16)
        for c in range(vta_ref.shape[0]):
            vta_ref[c] = vta[:, c * KV_SUB:(c + 1) * KV_SUB]
            vtb_ref[c] = vtb[:, c * KV_SUB:(c + 1) * KV_SUB]


def _in_proj(x2, g, w_big, w_t, w_small, tm):
    m, d = x2.shape
    last = PROJ_STEPS - 1
    wide = N_COL_BLOCKS * d // PROJ_STEPS
    assert wide % LANES == 0 and wide * PROJ_STEPS == N_COL_BLOCKS * d
    sub = tm // KV_SUB
    vt_shape = jax.ShapeDtypeStruct((m // KV_SUB, d, KV_SUB), BF16)
    return pl.pallas_call(
        _in_proj_kernel,
        grid=(m // tm, PROJ_STEPS + 1),
        in_specs=[
            pl.BlockSpec((tm, d), lambda i, j: (i, 0)),
            pl.BlockSpec((1, d), lambda i, j: (0, 0)),
            pl.BlockSpec((d, wide), lambda i, j: (0, jnp.minimum(j, last))),
            pl.BlockSpec((N_T_BLOCKS * d, d), lambda i, j: (0, 0)),
            pl.BlockSpec((d, LANES), lambda i, j: (0, 0)),
        ],
        out_specs=[
            pl.BlockSpec((tm, wide), lambda i, j: (i, jnp.minimum(j, last))),
            pl.BlockSpec((tm, LANES), lambda i, j: (i, 0)),
            pl.BlockSpec((d, tm), lambda i, j: (0, i)),
            pl.BlockSpec((sub, d, KV_SUB), lambda i, j: (i, 0, 0)),
            pl.BlockSpec((sub, d, KV_SUB), lambda i, j: (i, 0, 0)),
        ],
        out_shape=[
            jax.ShapeDtypeStruct((m, N_COL_BLOCKS * d), BF16),
            jax.ShapeDtypeStruct((m, LANES), F32),
            jax.ShapeDtypeStruct((d, m), BF16),
            vt_shape,
            vt_shape,
        ],
        scratch_shapes=[pltpu.VMEM((tm, d), BF16)],
        compiler_params=pltpu.CompilerParams(
            dimension_semantics=("arbitrary", "arbitrary"), vmem_limit_bytes=VMEM_LIMIT),
        name="in_proj",
    )(x2, g, w_big, w_t, w_small)


def _fcumsum_kernel(s_ref, b_ref, gk_ref, gqt_ref):
    seqlen = s_ref.shape[0]
    blk = SSM_CHUNK
    tri = _lower_tri(blk).astype(BF16)
    slot = lax.broadcasted_iota(jnp.int32, (blk, LANES), 1) % 8
    carry = jnp.zeros((1, LANES), F32)
    for c in range(seqlen // blk):
        rows = slice(c * blk, (c + 1) * blk)
        raw = s_ref[rows, :] + b_ref[...]
        logf = jnp.where(slot == F_LANE, -_softplus(-raw), 0.0)
        f_blk = _tri_cumsum(tri, logf) + carry
        carry = f_blk[blk - 1:blk, :]
        hi, mid, lo = (p.astype(F32) for p in _split3(f_blk * LOG2E))
        gq = hi + pltpu.roll(mid, 1, axis=1) + pltpu.roll(lo, 2, axis=1) + jnp.where((slot >= 3) & (slot < 6), 1.0, 0.0)
        gk = jnp.where(slot < 3, 1.0, 0.0) - (pltpu.roll(hi, 3, axis=1) + pltpu.roll(mid, 4, axis=1)
                                              + pltpu.roll(lo, 5, axis=1))
        gk_ref[rows, :] = gk.astype(BF16)
        gqt_ref[:, rows] = gq.T.astype(BF16)


def _fcumsum(small, fb_vec, bsz, seqlen):
    return pl.pallas_call(
        _fcumsum_kernel,
        grid=(bsz,),
        in_specs=[
            pl.BlockSpec((seqlen, LANES), lambda b: (b, 0)),
            pl.BlockSpec((1, LANES), lambda b: (0, 0)),
        ],
        out_specs=[
            pl.BlockSpec((seqlen, LANES), lambda b: (b, 0)),
            pl.BlockSpec((LANES, seqlen), lambda b: (b, 0)),
        ],
        out_shape=[
            jax.ShapeDtypeStruct((bsz * seqlen, LANES), BF16),
            jax.ShapeDtypeStruct((bsz * LANES, seqlen), BF16),
        ],
        compiler_params=pltpu.CompilerParams(
            dimension_semantics=("arbitrary",), vmem_limit_bytes=VMEM_LIMIT),
        name="fcumsum",
    )(small, fb_vec)


def _attn_kernel(qt_ref, k_ref, gk_ref, vta_ref, vtb_ref, gqt_ref, o_ref,
                 qaug_ref, s_a0, s_a1, s_b0, s_b1, acc_ref, m_ref, *, tq):
    hp = pl.program_id(1)
    i = pl.program_id(2)
    tk = KV_SUB
    sub_per_q = tq // tk
    s_refs = ((s_a0, s_a1), (s_b0, s_b1))
    vt_refs = (vta_ref, vtb_ref)

    grp = lax.broadcasted_iota(jnp.int32, (LANES, tq), 0) // 8
    gq_t = gqt_ref[...].astype(F32)
    zeros_half = jnp.zeros((HEAD_DIM, tq), BF16)
    qaug_ref[0, :HEAD_DIM] = qt_ref[:HEAD_DIM, :]
    qaug_ref[0, HEAD_DIM:LANES] = zeros_half
    qaug_ref[1, :HEAD_DIM] = zeros_half
    qaug_ref[1, HEAD_DIM:LANES] = qt_ref[HEAD_DIM:, :]
    for h in range(2):
        qaug_ref[h, LANES:] = jnp.where(grp == 2 * hp + h, gq_t, 0.0).astype(BF16)
    acc_ref[...] = jnp.zeros(acc_ref.shape, F32)
    m_ref[...] = jnp.full(m_ref.shape, -jnp.inf, F32)

    def qk(sub, n0, slot):
        rows = pl.ds(pl.multiple_of(sub * tk, tk), tk)
        k_sub = jnp.concatenate([k_ref[rows, :], gk_ref[rows, :]], axis=1)
        for h in range(2):
            s_refs[h][slot][:, n0:] = jnp.dot(k_sub, qaug_ref[h, :, n0:], preferred_element_type=F32)

    def update(sub, n0, slot, keep=None):
        for h in range(2):
            s = s_refs[h][slot][:, n0:]
            if keep is not None:
                diag = jnp.where(keep, s[:, :tk], -jnp.inf)
                s = diag if n0 + tk == tq else jnp.concatenate([diag, s[:, tk:]], axis=1)
            m = m_ref[h, 0:1, n0:]
            mn = jnp.maximum(m, jnp.max(s, axis=0, keepdims=True))
            p = jnp.exp2(s - mn).astype(BF16)
            acc_ref[h, :, n0:] = (acc_ref[h, :, n0:] * jnp.exp2(m - mn)
                                  + jnp.dot(vt_refs[h][sub], p, preferred_element_type=F32))
            m_ref[h, 0:1, n0:] = mn

    first_diag = i * sub_per_q
    qk(0, 0, 0)

    def body(j, carry):
        for c in range(sub_per_q):
            sub = j * sub_per_q + c
            qk(sub + 1, 0, (c + 1) % 2)
            update(sub, 0, c % 2)
        return carry

    lax.fori_loop(0, i, body, 0)

    keep = lax.broadcasted_iota(jnp.int32, (tk, tk), 0) <= lax.broadcasted_iota(jnp.int32, (tk, tk), 1)
    for c in range(sub_per_q):
        n0 = c * tk
        if c + 1 < sub_per_q:
            qk(first_diag + c + 1, n0 + tk, (c + 1) % 2)
        update(first_diag + c, n0, c % 2, keep)

    inv_a = 1.0 / acc_ref[0, HEAD_DIM:HEAD_DIM + 1, :]
    inv_b = 1.0 / acc_ref[1, 0:1, :]
    o_t = jnp.concatenate([acc_ref[0, :HEAD_DIM, :] * inv_a, acc_ref[1, HEAD_DIM:, :] * inv_b], axis=0)
    o_ref[...] = o_t.T.astype(BF16)


def _attention(proj, q_t, vta, vtb, gk, gq_t, bsz, seqlen, d_model, tq):
    assert (tq // KV_SUB) % 2 == 0, "score buffers alternate per key sub-tile"
    n_hp = d_model // LANES
    per_row = d_model // LANES
    nq = seqlen // tq
    n_sub = seqlen // KV_SUB
    score_buf = pltpu.VMEM((KV_SUB, tq), F32)
    return pl.pallas_call(
        functools.partial(_attn_kernel, tq=tq),
        grid=(bsz, n_hp, nq),
        in_specs=[
            pl.BlockSpec((LANES, tq), lambda b, h, i: (h, b * nq + i)),
            pl.BlockSpec((seqlen, LANES), lambda b, h, i: (b, COL_K * per_row + h)),
            pl.BlockSpec((seqlen, LANES), lambda b, h, i: (b, 0)),
            pl.BlockSpec((n_sub, LANES, KV_SUB), lambda b, h, i: (b, h, 0)),
            pl.BlockSpec((n_sub, LANES, KV_SUB), lambda b, h, i: (b, h, 0)),
            pl.BlockSpec((LANES, tq), lambda b, h, i: (b, i)),
        ],
        out_specs=pl.BlockSpec((tq, LANES), lambda b, h, i: (b * nq + i, h)),
        out_shape=jax.ShapeDtypeStruct((bsz * seqlen, d_model), BF16),
        scratch_shapes=[
            pltpu.VMEM((2, 2 * LANES, tq), BF16),
            score_buf, score_buf, score_buf, score_buf,
            pltpu.VMEM((2, LANES, tq), F32),
            pltpu.VMEM((2, 8, tq), F32),
        ],
        compiler_params=pltpu.CompilerParams(
            dimension_semantics=("arbitrary", "arbitrary", "arbitrary"), vmem_limit_bytes=VMEM_LIMIT),
        name="fox_attention",
    )(q_t, proj, gk, vta, vtb, gq_t)


def _ssd_chunk(is_first, xs_ref, bc_ref, s_ref, cw_ref, cb_ref, dtb_ref, alog_ref, dskip_ref,
               expand_ref, shift_ref, y_ref, prev_ref, st_ref, n_heads):
    q = SSM_CHUNK
    d_inner = xs_ref.shape[1]
    gn = SSM_GROUPS * SSM_STATE
    heads_per_group = n_heads // SSM_GROUPS
    gw = heads_per_group * HEAD_DIM

    if is_first is not None:
        @pl.when(is_first)
        def _():
            prev_ref[...] = jnp.zeros(prev_ref.shape, BF16)
            st_ref[...] = jnp.zeros(st_ref.shape, F32)

    cur = jnp.concatenate([xs_ref[...], bc_ref[...]], axis=1)
    both = jnp.concatenate([cur, prev_ref[...]], axis=0)
    prev_ref[...] = cur
    acts = []
    for n in range(cur.shape[1] // CONV_COLS):
        cols = slice(n * CONV_COLS, (n + 1) * CONV_COLS)
        shifted = jnp.dot(shift_ref[...], both[:, cols], preferred_element_type=F32)
        conv = cb_ref[:, cols] + cw_ref[SSM_CONV - 1:SSM_CONV, cols] * cur[:, cols].astype(F32)
        for jj in range(SSM_CONV - 1):
            k = SSM_CONV - 1 - jj
            conv = conv + cw_ref[jj:jj + 1, cols] * shifted[(k - 1) * q:k * q, :]
        acts.append(_silu(conv))
        yield
    act = jnp.concatenate(acts, axis=1)
    xs = act[:, :d_inner]
    b16 = act[:, d_inner:d_inner + gn].astype(BF16)
    c16 = act[:, d_inner + gn:].astype(BF16)

    lane = lax.broadcasted_iota(jnp.int32, (q, LANES), 1)
    head_lane = lane % 8 == DT_LANE
    dt = jnp.where(head_lane, _softplus(s_ref[...] + dtb_ref[...]), 0.0)
    da = dt * (-jnp.exp(alog_ref[...]))
    tri = _lower_tri(q)
    cs = _tri_cumsum(tri.astype(BF16), da)
    yield
    cs2 = cs * LOG2E
    key_t = (cs2 - jnp.log2(dt)).T
    last = cs[q - 1:q, :]
    w_end = dt * jnp.exp(last - cs)
    ecs = jnp.exp(cs)
    yield

    def expand(v):
        hi, mid, _ = _split3(jnp.where(head_lane, v, 0.0))
        both = hi.astype(F32) + pltpu.roll(mid.astype(F32), 1, axis=1)
        return jnp.dot(both.astype(BF16), expand_ref[...], preferred_element_type=F32)

    w_e = expand(w_end)
    ecs_e = expand(ecs)
    xw16 = (xs * w_e).astype(BF16)
    yield

    xs16 = xs.astype(BF16)
    first_half = lax.broadcasted_iota(jnp.int32, (q, LANES), 1) < HEAD_DIM

    y_parts = []
    for g in range(SSM_GROUPS):
        bg = b16[:, g * SSM_STATE:(g + 1) * SSM_STATE]
        cg = c16[:, g * SSM_STATE:(g + 1) * SSM_STATE]
        cb = lax.dot_general(cg, bg, (((1,), (1,)), ((), ())), preferred_element_type=F32)
        st_prev = st_ref[g]
        y_off = jnp.dot(cg, st_prev.astype(BF16), preferred_element_type=F32)
        yield
        for pair in range(heads_per_group // 2):
            lo_lane = g * gw + pair * LANES
            halves = []
            for half in range(2):
                h = 8 * (g * heads_per_group + pair * 2 + half) + DT_LANE
                seg = cs2[:, h:h + 1] - key_t[h:h + 1, :]
                mmat = (cb * jnp.exp2(jnp.where(tri, seg, -jnp.inf))).astype(BF16)
                halves.append(jnp.dot(mmat, xs16[:, lo_lane:lo_lane + LANES], preferred_element_type=F32))
            y_pair = jnp.where(first_half, halves[0], halves[1])
            y_pair = y_pair + y_off[:, pair * LANES:(pair + 1) * LANES] * ecs_e[:, lo_lane:lo_lane + LANES]
            y_parts.append(y_pair)
            yield
        bg_t = bg.astype(F32).T.astype(BF16)
        new_st = jnp.dot(bg_t, xw16[:, g * gw:(g + 1) * gw], preferred_element_type=F32)
        st_ref[g] = st_prev * ecs_e[q - 1:q, g * gw:(g + 1) * gw] + new_st
        yield

    y_ref[...] = (jnp.concatenate(y_parts, axis=1) + xs * dskip_ref[...]).astype(BF16)
    yield


def _ssd_kernel(xs_ref, bc_ref, s_ref, cw_ref, cb_ref, dtb_ref, alog_ref, dskip_ref,
                expand_ref, shift_ref, y_ref, prev_ref, st_ref, *, n_heads):
    c = pl.program_id(1)
    q = SSM_CHUNK
    for cc in range(xs_ref.shape[1] // q):
        rows = slice(cc * q, (cc + 1) * q)
        is_first = (c == 0) if cc == 0 else None
        chunks = [_ssd_chunk(is_first, xs_ref.at[seq, rows], bc_ref.at[seq, rows], s_ref.at[seq, rows], cw_ref, cb_ref,
                             dtb_ref, alog_ref, dskip_ref, expand_ref, shift_ref, y_ref.at[seq, rows],
                             prev_ref.at[seq], st_ref.at[seq], n_heads)
                  for seq in range(xs_ref.shape[0])]
        for _ in zip(*chunks):
            pass


def _ssd(proj, small, conv_w, conv_b, dtb_vec, alog_vec, dskip_e, expand_m, shift_m, bsz, seqlen, d_model, n_heads):
    q = SSM_CHUNK
    conv_dim = conv_w.shape[1]
    gw = (n_heads // SSM_GROUPS) * HEAD_DIM
    nb = 2 if bsz % 2 == 0 else 1
    rows = SSD_CHUNKS_PER_STEP * q
    assert seqlen % rows == 0
    proj3 = proj.reshape(bsz, seqlen, proj.shape[1])
    const = lambda b, c: (0, 0)
    y = pl.pallas_call(
        functools.partial(_ssd_kernel, n_heads=n_heads),
        grid=(bsz // nb, seqlen // rows),
        in_specs=[
            pl.BlockSpec((nb, rows, d_model), lambda b, c: (b, c, COL_XS)),
            pl.BlockSpec((nb, rows, d_model), lambda b, c: (b, c, COL_BC)),
            pl.BlockSpec((nb, rows, LANES), lambda b, c: (b, c, 0)),
            pl.BlockSpec((SSM_CONV, conv_dim), const),
            pl.BlockSpec((1, conv_dim), const),
            pl.BlockSpec((1, LANES), const),
            pl.BlockSpec((1, LANES), const),
            pl.BlockSpec((1, d_model), const),
            pl.BlockSpec((LANES, d_model), const),
            pl.BlockSpec(((SSM_CONV - 1) * q, 2 * q), const),
        ],
        out_specs=pl.BlockSpec((nb, rows, d_model), lambda b, c: (b, c, 0)),
        out_shape=jax.ShapeDtypeStruct((bsz, seqlen, d_model), BF16),
        scratch_shapes=[
            pltpu.VMEM((nb, q, conv_dim), BF16),
            pltpu.VMEM((nb, SSM_GROUPS, SSM_STATE, gw), F32),
        ],
        compiler_params=pltpu.CompilerParams(
            dimension_semantics=("arbitrary", "arbitrary"), vmem_limit_bytes=VMEM_LIMIT),
        name="ssd_branch",
    )(proj3, proj3, small.reshape(bsz, seqlen, LANES), conv_w, conv_b, dtb_vec, alog_vec, dskip_e, expand_m, shift_m)
    return y.reshape(bsz * seqlen, d_model)


def _half_silu_gate(v16, zh16):
    return v16 * zh16 * (jnp.tanh(zh16) + 1.0)


def _merge_kernel(ys_ref, zs_ref, oa_ref, za_ref, gs_ref, ga_ref, x_ref, ws_ref, wa_ref, wo_ref, pg_ref, o_ref):
    y_att = _half_silu_gate(oa_ref[...], za_ref[...])
    p_att = jnp.dot(y_att, wa_ref[...], preferred_element_type=F32)
    yg = _half_silu_gate(ys_ref[...], zs_ref[...]).astype(F32)
    gw = yg.shape[1] // SSM_GROUPS
    normed = []
    for g in range(SSM_GROUPS):
        blk = yg[:, g * gw:(g + 1) * gw]
        normed.append(blk * lax.rsqrt(jnp.mean(blk * blk, axis=-1, keepdims=True) + RMS_EPS))
    y_ssm = jnp.concatenate(normed, axis=1).astype(BF16)
    p_ssm = jnp.dot(y_ssm, ws_ref[...], preferred_element_type=F32)
    gate_s = (jnp.tanh(gs_ref[...]) + 1.0).astype(F32)
    gate_a = (jnp.tanh(ga_ref[...]) + 1.0).astype(F32)
    merged = gate_s * p_ssm + gate_a * p_att
    out = jnp.dot(merged.astype(BF16), wo_ref[...], preferred_element_type=F32)
    ms = jnp.mean(out * out, axis=-1, keepdims=True)
    o_ref[...] = x_ref[...] + out * lax.rsqrt(ms + RMS_EPS) * pg_ref[...]


def _merge(y_scan, o_att, proj, x2, w_ssm, w_att, w_out, post_g, tm):
    m, d = x2.shape
    row = lambda i: (i, 0)
    full = lambda i: (0, 0)
    return pl.pallas_call(
        _merge_kernel,
        grid=(m // tm,),
        in_specs=[
            pl.BlockSpec((tm, d), row),
            pl.BlockSpec((tm, d), lambda i: (i, COL_Z_SSM)),
            pl.BlockSpec((tm, d), row),
            pl.BlockSpec((tm, d), lambda i: (i, COL_Z_ATT)),
            pl.BlockSpec((tm, d), lambda i: (i, COL_G_SSM)),
            pl.BlockSpec((tm, d), lambda i: (i, COL_G_ATT)),
            pl.BlockSpec((tm, d), row),
            pl.BlockSpec((d, d), full),
            pl.BlockSpec((d, d), full),
            pl.BlockSpec((d, d), full),
            pl.BlockSpec((1, d), full),
        ],
        out_specs=pl.BlockSpec((tm, d), row),
        out_shape=jax.ShapeDtypeStruct((m, d), F32),
        compiler_params=pltpu.CompilerParams(
            dimension_semantics=("arbitrary",), vmem_limit_bytes=VMEM_LIMIT),
        name="gated_merge",
    )(y_scan, proj, o_att, proj, proj, proj, x2, w_ssm, w_att, w_out, post_g)


def _head_lanes(v, slot):
    cols = [jnp.zeros_like(v, dtype=F32)] * 8
    cols[slot] = v.astype(F32)
    return jnp.stack(cols, axis=-1).reshape(1, LANES)


def _layer(x, pre_g, w_in, conv_w, conv_b, dt_bias, a_log, d_skip, ssm_norm_g, fgate_b,
           w_branch_ssm, w_branch_att, w_out, post_g):
    bsz, seqlen, d = x.shape
    n_heads = d // HEAD_DIM
    assert n_heads * 8 == LANES, "small-projection lane layout assumes 16 heads"
    gn = SSM_GROUPS * SSM_STATE
    sizes = (d, d + 2 * gn, n_heads, d, d, d, n_heads, d, d, d)
    offs = [0]
    for sz in sizes:
        offs.append(offs[-1] + sz)
    z_ssm, xbc, dt_w, q_w, k_w, v_w, f_w, z_att, g_ssm, g_att = (w_in[:, offs[n]:offs[n + 1]] for n in range(10))
    w_big = jnp.concatenate([0.5 * z_ssm, xbc, k_w, 0.5 * z_att, 0.5 * g_ssm, 0.5 * g_att], axis=1).astype(BF16)
    w_t = jnp.concatenate([(q_w * (HEAD_DIM ** -0.5 * LOG2E)).T, v_w.T], axis=0).astype(BF16)
    slots = [jnp.zeros_like(f_w)] * 8
    slots[F_LANE], slots[DT_LANE] = f_w, dt_w
    w_small = jnp.stack(slots, axis=-1).reshape(d, LANES).astype(BF16)

    x2 = x.reshape(bsz * seqlen, d)
    proj, small, q_t, vta, vtb = _in_proj(x2, pre_g.reshape(1, d), w_big, w_t, w_small, tm=min(1024, bsz * seqlen))

    gk, gq_t = _fcumsum(small, _head_lanes(fgate_b, F_LANE), bsz, seqlen)
    o_att = _attention(proj, q_t, vta, vtb, gk, gq_t, bsz, seqlen, d, tq=min(2048, seqlen))

    head_of_col = np.arange(d) // HEAD_DIM
    lane_ids = np.arange(LANES)[:, None]
    expand_m = jnp.asarray((lane_ids == 8 * head_of_col[None, :] + DT_LANE)
                           | (lane_ids == 8 * head_of_col[None, :] + DT_LANE + 1), BF16)
    t_ids = np.arange(SSM_CHUNK)[:, None]
    src = np.arange(2 * SSM_CHUNK)[None, :]
    shift_m = jnp.asarray(np.concatenate([src == np.where(t_ids >= k, t_ids - k, 2 * SSM_CHUNK + t_ids - k)
                                          for k in range(1, SSM_CONV)], axis=0), BF16)
    y_scan = _ssd(proj, small, conv_w, conv_b.reshape(1, -1), _head_lanes(dt_bias, DT_LANE), _head_lanes(a_log, DT_LANE),
                 jnp.repeat(d_skip.astype(F32), HEAD_DIM).reshape(1, d), expand_m, shift_m,
                 bsz, seqlen, d, n_heads)

    out = _merge(y_scan, o_att, proj, x2, (ssm_norm_g.astype(F32)[:, None] * w_branch_ssm).astype(BF16),
                 w_branch_att.astype(BF16), (0.5 * w_out).astype(BF16), post_g.reshape(1, d), tm=min(512, bsz * seqlen))
    return out.reshape(bsz, seqlen, d)


def kernel(x, pre_norm_g, w_in, conv_w, conv_b, dt_bias, a_log, d_skip, ssm_norm_g, fgate_b,
           w_branch_ssm, w_branch_att, w_out, post_norm_g):
    for i in range(pre_norm_g.shape[0]):
        x = _layer(x, pre_norm_g[i], w_in[i], conv_w[i], conv_b[i], dt_bias[i], a_log[i], d_skip[i],
                   ssm_norm_g[i], fgate_b[i], w_branch_ssm[i], w_branch_att[i], w_out[i], post_norm_g[i])
    return x
```

```python
import functools

import jax
import jax.numpy as jnp
import numpy as np
from jax import lax
from jax.experimental import pallas as pl
from jax.experimental.pallas import tpu as pltpu

F32 = jnp.float32
BF16 = jnp.bfloat16

RMS_EPS = 1e-6
HEAD_DIM = 64
SSM_GROUPS = 4
SSM_STATE = 128
SSM_CONV = 4
SSM_CHUNK = 128
CONV_COLS = 512
SSD_CHUNKS_PER_STEP = 4
NORM_ROWS = 256
KV_SUB = 256
ATTN_PAIRS_PER_STEP = 2
LANES = 128
VMEM_LIMIT = 48 * 1024 * 1024

COL_Z_SSM, COL_XS, COL_BC, COL_K, COL_Z_ATT, COL_G_SSM, COL_G_ATT = range(7)
N_COL_BLOCKS = 7
PROJ_STEPS = 4
N_T_BLOCKS = 2
LOG2E = 1.4426950408889634
F_LANE = 0
DT_LANE = 4


def _sigmoid(x):
    return 0.5 * jnp.tanh(0.5 * x) + 0.5


def _silu(x):
    return x * _sigmoid(x)


def _softplus(x):
    return jnp.maximum(x, 0.0) + jnp.log(1.0 + jnp.exp(-jnp.abs(x)))


def _split3(x):
    hi = x.astype(BF16)
    r1 = x - hi.astype(F32)
    mid = r1.astype(BF16)
    lo = (r1 - mid.astype(F32)).astype(BF16)
    return hi, mid, lo


def _tri_cumsum(tri_bf16, x):
    hi, mid, lo = _split3(x)
    dot = functools.partial(jnp.dot, preferred_element_type=F32)
    return dot(tri_bf16, hi) + dot(tri_bf16, mid) + dot(tri_bf16, lo)


def _lower_tri(n):
    r = lax.broadcasted_iota(jnp.int32, (n, n), 0)
    c = lax.broadcasted_iota(jnp.int32, (n, n), 1)
    return r >= c


def _in_proj_kernel(x_ref, g_ref, w_ref, wt_ref, ws_ref, p_ref, s_ref, qt_ref, vta_ref, vtb_ref, h_ref):
    j = pl.program_id(1)
    nt = (((1,), (1,)), ((), ()))

    @pl.when(j == 0)
    def _():
        for r0 in range(0, x_ref.shape[0], NORM_ROWS):
            rows = slice(r0, r0 + NORM_ROWS)
            x = x_ref[rows, :]
            ms = jnp.mean(x * x, axis=-1, keepdims=True)
            h = (x * lax.rsqrt(ms + RMS_EPS) * g_ref[...]).astype(BF16)
            h_ref[rows, :] = h
            s_ref[rows, :] = jnp.dot(h, ws_ref[...], preferred_element_type=F32)
            p_ref[rows, :] = jnp.dot(h, w_ref[...], preferred_element_type=F32).astype(BF16)

    @pl.when((j > 0) & (j < PROJ_STEPS))
    def _():
        p_ref[...] = jnp.dot(h_ref[...], w_ref[...], preferred_element_type=F32).astype(BF16)

    @pl.when(j == PROJ_STEPS)
    def _():
        d = qt_ref.shape[0]
        qt_ref[...] = lax.dot_general(wt_ref[:d, :], h_ref[...], nt, preferred_element_type=F32).astype(BF16)
        vt = lax.dot_general(wt_ref[d:, :], h_ref[...], nt, preferred_element_type=F32)
        first = (lax.broadcasted_iota(jnp.int32, vt.shape, 0) % LANES) < HEAD_DIM
        vta = jnp.where(first, vt, 1.0).astype(BF16)
        vtb = jnp.where(first, 1.0, vt).astype(BF16)
        for c in range(vta_ref.shape[0]):
            vta_ref[c] = vta[:, c * KV_SUB:(c + 1) * KV_SUB]
            vtb_ref[c] = vtb[:, c * KV_SUB:(c + 1) * KV_SUB]


def _in_proj(x2, g, w_big, w_t, w_small, tm):
    m, d = x2.shape
    last = PROJ_STEPS - 1
    wide = N_COL_BLOCKS * d // PROJ_STEPS
    assert wide % LANES == 0 and wide * PROJ_STEPS == N_COL_BLOCKS * d
    sub = tm // KV_SUB
    vt_shape = jax.ShapeDtypeStruct((m // KV_SUB, d, KV_SUB), BF16)
    return pl.pallas_call(
        _in_proj_kernel,
        grid=(m // tm, PROJ_STEPS + 1),
        in_specs=[
            pl.BlockSpec((tm, d), lambda i, j: (i, 0)),
            pl.BlockSpec((1, d), lambda i, j: (0, 0)),
            pl.BlockSpec((d, wide), lambda i, j: (0, jnp.minimum(j, last))),
            pl.BlockSpec((N_T_BLOCKS * d, d), lambda i, j: (0, 0)),
            pl.BlockSpec((d, LANES), lambda i, j: (0, 0)),
        ],
        out_specs=[
            pl.BlockSpec((tm, wide), lambda i, j: (i, jnp.minimum(j, last))),
            pl.BlockSpec((tm, LANES), lambda i, j: (i, 0)),
            pl.BlockSpec((d, tm), lambda i, j: (0, i)),
            pl.BlockSpec((sub, d, KV_SUB), lambda i, j: (i, 0, 0)),
            pl.BlockSpec((sub, d, KV_SUB), lambda i, j: (i, 0, 0)),
        ],
        out_shape=[
            jax.ShapeDtypeStruct((m, N_COL_BLOCKS * d), BF16),
            jax.ShapeDtypeStruct((m, LANES), F32),
            jax.ShapeDtypeStruct((d, m), BF16),
            vt_shape,
            vt_shape,
        ],
        scratch_shapes=[pltpu.VMEM((tm, d), BF16)],
        compiler_params=pltpu.CompilerParams(
            dimension_semantics=("arbitrary", "arbitrary"), vmem_limit_bytes=VMEM_LIMIT),
        name="in_proj",
    )(x2, g, w_big, w_t, w_small)


def _fcumsum_kernel(s_ref, b_ref, gk_ref, gqt_ref):
    seqlen = s_ref.shape[0]
    blk = SSM_CHUNK
    tri = _lower_tri(blk).astype(BF16)
    slot = lax.broadcasted_iota(jnp.int32, (blk, LANES), 1) % 8
    carry = jnp.zeros((1, LANES), F32)
    for c in range(seqlen // blk):
        rows = slice(c * blk, (c + 1) * blk)
        raw = s_ref[rows, :] + b_ref[...]
        logf = jnp.where(slot == F_LANE, -_softplus(-raw), 0.0)
        f_blk = _tri_cumsum(tri, logf) + carry
        carry = f_blk[blk - 1:blk, :]
        hi, mid, lo = (p.astype(F32) for p in _split3(f_blk * LOG2E))
        gq = hi + pltpu.roll(mid, 1, axis=1) + pltpu.roll(lo, 2, axis=1) + jnp.where((slot >= 3) & (slot < 6), 1.0, 0.0)
        gk = jnp.where(slot < 3, 1.0, 0.0) - (pltpu.roll(hi, 3, axis=1) + pltpu.roll(mid, 4, axis=1)
                                              + pltpu.roll(lo, 5, axis=1))
        gk_ref[rows, :] = gk.astype(BF16)
        gqt_ref[:, rows] = gq.T.astype(BF16)


def _fcumsum(small, fb_vec, bsz, seqlen):
    return pl.pallas_call(
        _fcumsum_kernel,
        grid=(bsz,),
        in_specs=[
            pl.BlockSpec((seqlen, LANES), lambda b: (b, 0)),
            pl.BlockSpec((1, LANES), lambda b: (0, 0)),
        ],
        out_specs=[
            pl.BlockSpec((seqlen, LANES), lambda b: (b, 0)),
            pl.BlockSpec((LANES, seqlen), lambda b: (b, 0)),
        ],
        out_shape=[
            jax.ShapeDtypeStruct((bsz * seqlen, LANES), BF16),
            jax.ShapeDtypeStruct((bsz * LANES, seqlen), BF16),
        ],
        compiler_params=pltpu.CompilerParams(
            dimension_semantics=("arbitrary",), vmem_limit_bytes=VMEM_LIMIT),
        name="fcumsum",
    )(small, fb_vec)


def _attn_kernel(qt_ref, k_ref, gk_ref, vta_ref, vtb_ref, gqt_ref, o_ref,
                 qaug_ref, s_a0, s_a1, s_b0, s_b1, acc_ref, m_ref, *, tq):
    i = pl.program_id(2)
    for pair in range(qt_ref.shape[0] // LANES):
        lanes = slice(pair * LANES, (pair + 1) * LANES)
        _attn_pair(ATTN_PAIRS_PER_STEP * pl.program_id(1) + pair, i, qt_ref.at[lanes, :], k_ref.at[:, lanes], gk_ref,
                   vta_ref.at[:, lanes, :], vtb_ref.at[:, lanes, :], gqt_ref, o_ref.at[:, lanes],
                   qaug_ref, ((s_a0, s_a1), (s_b0, s_b1)), acc_ref, m_ref, tq)


def _attn_pair(hp, i, qt_ref, k_ref, gk_ref, vta_ref, vtb_ref, gqt_ref, o_ref, qaug_ref, s_refs, acc_ref, m_ref, tq):
    tk = KV_SUB
    sub_per_q = tq // tk
    vt_refs = (vta_ref, vtb_ref)

    grp = lax.broadcasted_iota(jnp.int32, (LANES, tq), 0) // 8
    gq_t = gqt_ref[...].astype(F32)
    zeros_half = jnp.zeros((HEAD_DIM, tq), BF16)
    qaug_ref[0, :HEAD_DIM] = qt_ref[:HEAD_DIM, :]
    qaug_ref[0, HEAD_DIM:LANES] = zeros_half
    qaug_ref[1, :HEAD_DIM] = zeros_half
    qaug_ref[1, HEAD_DIM:LANES] = qt_ref[HEAD_DIM:, :]
    for h in range(2):
        qaug_ref[h, LANES:] = jnp.where(grp == 2 * hp + h, gq_t, 0.0).astype(BF16)
    acc_ref[...] = jnp.zeros(acc_ref.shape, F32)
    m_ref[...] = jnp.full(m_ref.shape, -jnp.inf, F32)

    def qk(sub, n0, slot):
        rows = pl.ds(pl.multiple_of(sub * tk, tk), tk)
        k_sub = jnp.concatenate([k_ref[rows, :], gk_ref[rows, :]], axis=1)
        for h in range(2):
            s_refs[h][slot][:, n0:] = jnp.dot(k_sub, qaug_ref[h, :, n0:], preferred_element_type=F32)

    def update(sub, n0, slot, keep=None):
        for h in range(2):
            s = s_refs[h][slot][:, n0:]
            if keep is not None:
                diag = jnp.where(keep, s[:, :tk], -jnp.inf)
                s = diag if n0 + tk == tq else jnp.concatenate([diag, s[:, tk:]], axis=1)
            m = m_ref[h, 0:1, n0:]
            mn = jnp.maximum(m, jnp.max(s, axis=0, keepdims=True))
            p = jnp.exp2(s - mn).astype(BF16)
            acc_ref[h, :, n0:] = (acc_ref[h, :, n0:] * jnp.exp2(m - mn)
                                  + jnp.dot(vt_refs[h][sub], p, preferred_element_type=F32))
            m_ref[h, 0:1, n0:] = mn

    first_diag = i * sub_per_q
    qk(0, 0, 0)

    def body(j, carry):
        for c in range(sub_per_q):
            sub = j * sub_per_q + c
            qk(sub + 1, 0, (c + 1) % 2)
            update(sub, 0, c % 2)
        return carry

    lax.fori_loop(0, i, body, 0)

    keep = lax.broadcasted_iota(jnp.int32, (tk, tk), 0) <= lax.broadcasted_iota(jnp.int32, (tk, tk), 1)
    for c in range(sub_per_q):
        n0 = c * tk
        if c + 1 < sub_per_q:
            qk(first_diag + c + 1, n0 + tk, (c + 1) % 2)
        update(first_diag + c, n0, c % 2, keep)

    inv_a = 1.0 / acc_ref[0, HEAD_DIM:HEAD_DIM + 1, :]
    inv_b = 1.0 / acc_ref[1, 0:1, :]
    o_t = jnp.concatenate([acc_ref[0, :HEAD_DIM, :] * inv_a, acc_ref[1, HEAD_DIM:, :] * inv_b], axis=0)
    o_ref[...] = o_t.T.astype(BF16)


def _attention(proj, q_t, vta, vtb, gk, gq_t, bsz, seqlen, d_model, tq):
    assert (tq // KV_SUB) % 2 == 0, "score buffers alternate per key sub-tile"
    pp = ATTN_PAIRS_PER_STEP
    n_hp = d_model // LANES
    assert n_hp % pp == 0
    per_row = n_hp // pp
    nq = seqlen // tq
    n_sub = seqlen // KV_SUB
    score_buf = pltpu.VMEM((KV_SUB, tq), F32)
    return pl.pallas_call(
        functools.partial(_attn_kernel, tq=tq),
        grid=(bsz, n_hp // pp, nq),
        in_specs=[
            pl.BlockSpec((pp * LANES, tq), lambda b, h, i: (h, b * nq + i)),
            pl.BlockSpec((seqlen, pp * LANES), lambda b, h, i: (b, COL_K * per_row + h)),
            pl.BlockSpec((seqlen, LANES), lambda b, h, i: (b, 0)),
            pl.BlockSpec((n_sub, pp * LANES, KV_SUB), lambda b, h, i: (b, h, 0)),
            pl.BlockSpec((n_sub, pp * LANES, KV_SUB), lambda b, h, i: (b, h, 0)),
            pl.BlockSpec((LANES, tq), lambda b, h, i: (b, i)),
        ],
        out_specs=pl.BlockSpec((tq, pp * LANES), lambda b, h, i: (b * nq + i, h)),
        out_shape=jax.ShapeDtypeStruct((bsz * seqlen, d_model), BF16),
        scratch_shapes=[
            pltpu.VMEM((2, 2 * LANES, tq), BF16),
            score_buf, score_buf, score_buf, score_buf,
            pltpu.VMEM((2, LANES, tq), F32),
            pltpu.VMEM((2, 8, tq), F32),
        ],
        compiler_params=pltpu.CompilerParams(
            dimension_semantics=("arbitrary", "arbitrary", "arbitrary"), vmem_limit_bytes=VMEM_LIMIT),
        name="fox_attention",
    )(q_t, proj, gk, vta, vtb, gq_t)


def _ssd_chunk(is_first, xs_ref, bc_ref, s_ref, cw_ref, cb_ref, dtb_ref, alog_ref, dskip_ref,
               expand_ref, shift_ref, y_ref, prev_ref, st_ref, n_heads):
    q = SSM_CHUNK
    d_inner = xs_ref.shape[1]
    gn = SSM_GROUPS * SSM_STATE
    heads_per_group = n_heads // SSM_GROUPS
    gw = heads_per_group * HEAD_DIM

    if is_first is not None:
        @pl.when(is_first)
        def _():
            prev_ref[...] = jnp.zeros(prev_ref.shape, BF16)
            st_ref[...] = jnp.zeros(st_ref.shape, F32)

    cur = jnp.concatenate([xs_ref[...], bc_ref[...]], axis=1)
    both = jnp.concatenate([cur, prev_ref[...]], axis=0)
    prev_ref[...] = cur
    acts = []
    for n in range(cur.shape[1] // CONV_COLS):
        cols = slice(n * CONV_COLS, (n + 1) * CONV_COLS)
        shifted = jnp.dot(shift_ref[...], both[:, cols], preferred_element_type=F32)
        conv = cb_ref[:, cols] + cw_ref[SSM_CONV - 1:SSM_CONV, cols] * cur[:, cols].astype(F32)
        for jj in range(SSM_CONV - 1):
            k = SSM_CONV - 1 - jj
            conv = conv + cw_ref[jj:jj + 1, cols] * shifted[(k - 1) * q:k * q, :]
        acts.append(_silu(conv))
        yield
    act = jnp.concatenate(acts, axis=1)
    xs = act[:, :d_inner]
    b16 = act[:, d_inner:d_inner + gn].astype(BF16)
    c16 = act[:, d_inner + gn:].astype(BF16)

    lane = lax.broadcasted_iota(jnp.int32, (q, LANES), 1)
    head_lane = lane % 8 == DT_LANE
    dt = jnp.where(head_lane, _softplus(s_ref[...] + dtb_ref[...]), 0.0)
    da = dt * (-jnp.exp(alog_ref[...]))
    tri = _lower_tri(q)
    cs = _tri_cumsum(tri.astype(BF16), da)
    yield
    cs2 = cs * LOG2E
    key_t = (cs2 - jnp.log2(dt)).T
    last = cs[q - 1:q, :]
    w_end = dt * jnp.exp(last - cs)
    ecs = jnp.exp(cs)
    yield

    def expand(v):
        hi, mid, _ = _split3(jnp.where(head_lane, v, 0.0))
        both = hi.astype(F32) + pltpu.roll(mid.astype(F32), 1, axis=1)
        return jnp.dot(both.astype(BF16), expand_ref[...], preferred_element_type=F32)

    w_e = expand(w_end)
    ecs_e = expand(ecs)
    xw16 = (xs * w_e).astype(BF16)
    yield

    xs16 = xs.astype(BF16)
    first_half = lax.broadcasted_iota(jnp.int32, (q, LANES), 1) < HEAD_DIM

    y_parts = []
    for g in range(SSM_GROUPS):
        bg = b16[:, g * SSM_STATE:(g + 1) * SSM_STATE]
        cg = c16[:, g * SSM_STATE:(g + 1) * SSM_STATE]
        cb = lax.dot_general(cg, bg, (((1,), (1,)), ((), ())), preferred_element_type=F32)
        st_prev = st_ref[g]
        y_off = jnp.dot(cg, st_prev.astype(BF16), preferred_element_type=F32)
        yield
        for pair in range(heads_per_group // 2):
            lo_lane = g * gw + pair * LANES
            halves = []
            for half in range(2):
                h = 8 * (g * heads_per_group + pair * 2 + half) + DT_LANE
                seg = cs2[:, h:h + 1] - key_t[h:h + 1, :]
                mmat = (cb * jnp.exp2(jnp.where(tri, seg, -jnp.inf))).astype(BF16)
                halves.append(jnp.dot(mmat, xs16[:, lo_lane:lo_lane + LANES], preferred_element_type=F32))
            y_pair = jnp.where(first_half, halves[0], halves[1])
            y_pair = y_pair + y_off[:, pair * LANES:(pair + 1) * LANES] * ecs_e[:, lo_lane:lo_lane + LANES]
            y_parts.append(y_pair)
            yield
        bg_t = bg.astype(F32).T.astype(BF16)
        new_st = jnp.dot(bg_t, xw16[:, g * gw:(g + 1) * gw], preferred_element_type=F32)
        st_ref[g] = st_prev * ecs_e[q - 1:q, g * gw:(g + 1) * gw] + new_st
        yield

    y_ref[...] = (jnp.concatenate(y_parts, axis=1) + xs * dskip_ref[...]).astype(BF16)
    yield


def _ssd_kernel(xs_ref, bc_ref, s_ref, cw_ref, cb_ref, dtb_ref, alog_ref, dskip_ref,
                expand_ref, shift_ref, y_ref, prev_ref, st_ref, *, n_heads):
    c = pl.program_id(1)
    q = SSM_CHUNK
    for cc in range(xs_ref.shape[1] // q):
        rows = slice(cc * q, (cc + 1) * q)
        is_first = (c == 0) if cc == 0 else None
        chunks = [_ssd_chunk(is_first, xs_ref.at[seq, rows], bc_ref.at[seq, rows], s_ref.at[seq, rows], cw_ref, cb_ref,
                             dtb_ref, alog_ref, dskip_ref, expand_ref, shift_ref, y_ref.at[seq, rows],
                             prev_ref.at[seq], st_ref.at[seq], n_heads)
                  for seq in range(xs_ref.shape[0])]
        for _ in zip(*chunks):
            pass


def _ssd(proj, small, conv_w, conv_b, dtb_vec, alog_vec, dskip_e, expand_m, shift_m, bsz, seqlen, d_model, n_heads):
    q = SSM_CHUNK
    conv_dim = conv_w.shape[1]
    gw = (n_heads // SSM_GROUPS) * HEAD_DIM
    nb = 2 if bsz % 2 == 0 else 1
    rows = SSD_CHUNKS_PER_STEP * q
    assert seqlen % rows == 0
    proj3 = proj.reshape(bsz, seqlen, proj.shape[1])
    const = lambda b, c: (0, 0)
    y = pl.pallas_call(
        functools.partial(_ssd_kernel, n_heads=n_heads),
        grid=(bsz // nb, seqlen // rows),
        in_specs=[
            pl.BlockSpec((nb, rows, d_model), lambda b, c: (b, c, COL_XS)),
            pl.BlockSpec((nb, rows, d_model), lambda b, c: (b, c, COL_BC)),
            pl.BlockSpec((nb, rows, LANES), lambda b, c: (b, c, 0)),
            pl.BlockSpec((SSM_CONV, conv_dim), const),
            pl.BlockSpec((1, conv_dim), const),
            pl.BlockSpec((1, LANES), const),
            pl.BlockSpec((1, LANES), const),
            pl.BlockSpec((1, d_model), const),
            pl.BlockSpec((LANES, d_model), const),
            pl.BlockSpec(((SSM_CONV - 1) * q, 2 * q), const),
        ],
        out_specs=pl.BlockSpec((nb, rows, d_model), lambda b, c: (b, c, 0)),
        out_shape=jax.ShapeDtypeStruct((bsz, seqlen, d_model), BF16),
        scratch_shapes=[
            pltpu.VMEM((nb, q, conv_dim), BF16),
            pltpu.VMEM((nb, SSM_GROUPS, SSM_STATE, gw), F32),
        ],
        compiler_params=pltpu.CompilerParams(
            dimension_semantics=("arbitrary", "arbitrary"), vmem_limit_bytes=VMEM_LIMIT),
        name="ssd_branch",
    )(proj3, proj3, small.reshape(bsz, seqlen, LANES), conv_w, conv_b, dtb_vec, alog_vec, dskip_e, expand_m, shift_m)
    return y.reshape(bsz * seqlen, d_model)


def _half_silu_gate(v16, zh16):
    return v16 * zh16 * (jnp.tanh(zh16) + 1.0)


def _merge_kernel(ys_ref, zs_ref, oa_ref, za_ref, gs_ref, ga_ref, x_ref, ws_ref, wa_ref, wo_ref, pg_ref, o_ref):
    y_att = _half_silu_gate(oa_ref[...], za_ref[...])
    p_att = jnp.dot(y_att, wa_ref[...], preferred_element_type=F32)
    yg = _half_silu_gate(ys_ref[...], zs_ref[...]).astype(F32)
    gw = yg.shape[1] // SSM_GROUPS
    normed = []
    for g in range(SSM_GROUPS):
        blk = yg[:, g * gw:(g + 1) * gw]
        normed.append(blk * lax.rsqrt(jnp.mean(blk * blk, axis=-1, keepdims=True) + RMS_EPS))
    y_ssm = jnp.concatenate(normed, axis=1).astype(BF16)
    p_ssm = jnp.dot(y_ssm, ws_ref[...], preferred_element_type=F32)
    gate_s = (jnp.tanh(gs_ref[...]) + 1.0).astype(F32)
    gate_a = (jnp.tanh(ga_ref[...]) + 1.0).astype(F32)
    merged = gate_s * p_ssm + gate_a * p_att
    out = jnp.dot(merged.astype(BF16), wo_ref[...], preferred_element_type=F32)
    ms = jnp.mean(out * out, axis=-1, keepdims=True)
    o_ref[...] = x_ref[...] + out * lax.rsqrt(ms + RMS_EPS) * pg_ref[...]


def _merge(y_scan, o_att, proj, x2, w_ssm, w_att, w_out, post_g, tm):
    m, d = x2.shape
    row = lambda i: (i, 0)
    full = lambda i: (0, 0)
    return pl.pallas_call(
        _merge_kernel,
        grid=(m // tm,),
        in_specs=[
            pl.BlockSpec((tm, d), row),
            pl.BlockSpec((tm, d), lambda i: (i, COL_Z_SSM)),
            pl.BlockSpec((tm, d), row),
            pl.BlockSpec((tm, d), lambda i: (i, COL_Z_ATT)),
            pl.BlockSpec((tm, d), lambda i: (i, COL_G_SSM)),
            pl.BlockSpec((tm, d), lambda i: (i, COL_G_ATT)),
            pl.BlockSpec((tm, d), row),
            pl.BlockSpec((d, d), full),
            pl.BlockSpec((d, d), full),
            pl.BlockSpec((d, d), full),
            pl.BlockSpec((1, d), full),
        ],
        out_specs=pl.BlockSpec((tm, d), row),
        out_shape=jax.ShapeDtypeStruct((m, d), F32),
        compiler_params=pltpu.CompilerParams(
            dimension_semantics=("arbitrary",), vmem_limit_bytes=VMEM_LIMIT),
        name="gated_merge",
    )(y_scan, proj, o_att, proj, proj, proj, x2, w_ssm, w_att, w_out, post_g)


def _head_lanes(v, slot):
    cols = [jnp.zeros_like(v, dtype=F32)] * 8
    cols[slot] = v.astype(F32)
    return jnp.stack(cols, axis=-1).reshape(1, LANES)


def _layer(x, pre_g, w_in, conv_w, conv_b, dt_bias, a_log, d_skip, ssm_norm_g, fgate_b,
           w_branch_ssm, w_branch_att, w_out, post_g):
    bsz, seqlen, d = x.shape
    n_heads = d // HEAD_DIM
    assert n_heads * 8 == LANES, "small-projection lane layout assumes 16 heads"
    gn = SSM_GROUPS * SSM_STATE
    sizes = (d, d + 2 * gn, n_heads, d, d, d, n_heads, d, d, d)
    offs = [0]
    for sz in sizes:
        offs.append(offs[-1] + sz)
    z_ssm, xbc, dt_w, q_w, k_w, v_w, f_w, z_att, g_ssm, g_att = (w_in[:, offs[n]:offs[n + 1]] for n in range(10))
    w_big = jnp.concatenate([0.5 * z_ssm, xbc, k_w, 0.5 * z_att, 0.5 * g_ssm, 0.5 * g_att], axis=1).astype(BF16)
    w_t = jnp.concatenate([(q_w * (HEAD_DIM ** -0.5 * LOG2E)).T, v_w.T], axis=0).astype(BF16)
    slots = [jnp.zeros_like(f_w)] * 8
    slots[F_LANE], slots[DT_LANE] = f_w, dt_w
    w_small = jnp.stack(slots, axis=-1).reshape(d, LANES).astype(BF16)

    x2 = x.reshape(bsz * seqlen, d)
    proj, small, q_t, vta, vtb = _in_proj(x2, pre_g.reshape(1, d), w_big, w_t, w_small, tm=min(1024, bsz * seqlen))

    gk, gq_t = _fcumsum(small, _head_lanes(fgate_b, F_LANE), bsz, seqlen)
    o_att = _attention(proj, q_t, vta, vtb, gk, gq_t, bsz, seqlen, d, tq=min(2048, seqlen))

    head_of_col = np.arange(d) // HEAD_DIM
    lane_ids = np.arange(LANES)[:, None]
    expand_m = jnp.asarray((lane_ids == 8 * head_of_col[None, :] + DT_LANE)
                           | (lane_ids == 8 * head_of_col[None, :] + DT_LANE + 1), BF16)
    t_ids = np.arange(SSM_CHUNK)[:, None]
    src = np.arange(2 * SSM_CHUNK)[None, :]
    shift_m = jnp.asarray(np.concatenate([src == np.where(t_ids >= k, t_ids - k, 2 * SSM_CHUNK + t_ids - k)
                                          for k in range(1, SSM_CONV)], axis=0), BF16)
    y_scan = _ssd(proj, small, conv_w, conv_b.reshape(1, -1), _head_lanes(dt_bias, DT_LANE), _head_lanes(a_log, DT_LANE),
                 jnp.repeat(d_skip.astype(F32), HEAD_DIM).reshape(1, d), expand_m, shift_m,
                 bsz, seqlen, d, n_heads)

    out = _merge(y_scan, o_att, proj, x2, (ssm_norm_g.astype(F32)[:, None] * w_branch_ssm).astype(BF16),
                 w_branch_att.astype(BF16), (0.5 * w_out).astype(BF16), post_g.reshape(1, d), tm=min(512, bsz * seqlen))
    return out.reshape(bsz, seqlen, d)


def kernel(x, pre_norm_g, w_in, conv_w, conv_b, dt_bias, a_log, d_skip, ssm_norm_g, fgate_b,
           w_branch_ssm, w_branch_att, w_out, post_norm_g):
    for i in range(pre_norm_g.shape[0]):
        x = _layer(x, pre_norm_g[i], w_in[i], conv_w[i], conv_b[i], dt_bias[i], a_log[i], d_skip[i],
                   ssm_norm_g[i], fgate_b[i], w_branch_ssm[i], w_branch_att[i], w_out[i], post_norm_g[i])
    return x
```

```python
import functools

import jax
import jax.numpy as jnp
import numpy as np
from jax import lax
from jax.experimental import pallas as pl
from jax.experimental.pallas import tpu as pltpu

F32 = jnp.float32
BF16 = jnp.bfloat16

RMS_EPS = 1e-6
HEAD_DIM = 64
SSM_GROUPS = 4
SSM_STATE = 128
SSM_CONV = 4
SSM_CHUNK = 128
CONV_COLS = 512
SSD_CHUNKS_PER_STEP = 4
NORM_ROWS = 256
KV_SUB = 256
LANES = 128
VMEM_LIMIT = 48 * 1024 * 1024

COL_Z_SSM, COL_XS, COL_BC, COL_K, COL_Z_ATT, COL_G_SSM, COL_G_ATT = range(7)
N_COL_BLOCKS = 7
PROJ_STEPS = 4
N_T_BLOCKS = 2
LOG2E = 1.4426950408889634
F_LANE = 0
DT_LANE = 4


def _sigmoid(x):
    return 0.5 * jnp.tanh(0.5 * x) + 0.5


def _silu(x):
    return x * _sigmoid(x)


def _softplus(x):
    return jnp.maximum(x, 0.0) + jnp.log(1.0 + jnp.exp(-jnp.abs(x)))


def _split3(x):
    hi = x.astype(BF16)
    r1 = x - hi.astype(F32)
    mid = r1.astype(BF16)
    lo = (r1 - mid.astype(F32)).astype(BF16)
    return hi, mid, lo


def _tri_cumsum(tri_bf16, x):
    hi, mid, lo = _split3(x)
    dot = functools.partial(jnp.dot, preferred_element_type=F32)
    return dot(tri_bf16, hi) + dot(tri_bf16, mid) + dot(tri_bf16, lo)


def _lower_tri(n):
    r = lax.broadcasted_iota(jnp.int32, (n, n), 0)
    c = lax.broadcasted_iota(jnp.int32, (n, n), 1)
    return r >= c


def _in_proj_kernel(x_ref, g_ref, w_ref, wt_ref, ws_ref, p_ref, s_ref, qt_ref, vta_ref, vtb_ref, h_ref):
    j = pl.program_id(1)
    nt = (((1,), (1,)), ((), ()))

    @pl.when(j == 0)
    def _():
        for r0 in range(0, x_ref.shape[0], NORM_ROWS):
            rows = slice(r0, r0 + NORM_ROWS)
            x = x_ref[rows, :]
            ms = jnp.mean(x * x, axis=-1, keepdims=True)
            h = (x * lax.rsqrt(ms + RMS_EPS) * g_ref[...]).astype(BF16)
            h_ref[rows, :] = h
            s_ref[rows, :] = jnp.dot(h, ws_ref[...], preferred_element_type=F32)
            p_ref[rows, :] = jnp.dot(h, w_ref[...], preferred_element_type=F32).astype(BF16)

    @pl.when((j > 0) & (j < PROJ_STEPS))
    def _():
        p_ref[...] = jnp.dot(h_ref[...], w_ref[...], preferred_element_type=F32).astype(BF16)

    @pl.when(j == PROJ_STEPS - 1)
    def _():
        d = qt_ref.shape[0]
        qt_ref[...] = lax.dot_general(wt_ref[:d, :], h_ref[...], nt, preferred_element_type=F32).astype(BF16)
        vt = lax.dot_general(wt_ref[d:, :], h_ref[...], nt, preferred_element_type=F32)
        first = (lax.broadcasted_iota(jnp.int32, vt.shape, 0) % LANES) < HEAD_DIM
        vta = jnp.where(first, vt, 1.0).astype(BF16)
        vtb = jnp.where(first, 1.0, vt).astype(BF16)
        for c in range(vta_ref.shape[0]):
            vta_ref[c] = vta[:, c * KV_SUB:(c + 1) * KV_SUB]
            vtb_ref[c] = vtb[:, c * KV_SUB:(c + 1) * KV_SUB]


def _in_proj(x2, g, w_big, w_t, w_small, tm):
    m, d = x2.shape
    last = PROJ_STEPS - 1
    wide = N_COL_BLOCKS * d // PROJ_STEPS
    assert wide % LANES == 0 and wide * PROJ_STEPS == N_COL_BLOCKS * d
    sub = tm // KV_SUB
    vt_shape = jax.ShapeDtypeStruct((m // KV_SUB, d, KV_SUB), BF16)
    return pl.pallas_call(
        _in_proj_kernel,
        grid=(m // tm, PROJ_STEPS),
        in_specs=[
            pl.BlockSpec((tm, d), lambda i, j: (i, 0)),
            pl.BlockSpec((1, d), lambda i, j: (0, 0)),
            pl.BlockSpec((d, wide), lambda i, j: (0, jnp.minimum(j, last))),
            pl.BlockSpec((N_T_BLOCKS * d, d), lambda i, j: (0, 0)),
            pl.BlockSpec((d, LANES), lambda i, j: (0, 0)),
        ],
        out_specs=[
            pl.BlockSpec((tm, wide), lambda i, j: (i, jnp.minimum(j, last))),
            pl.BlockSpec((tm, LANES), lambda i, j: (i, 0)),
            pl.BlockSpec((d, tm), lambda i, j: (0, i)),
            pl.BlockSpec((sub, d, KV_SUB), lambda i, j: (i, 0, 0)),
            pl.BlockSpec((sub, d, KV_SUB), lambda i, j: (i, 0, 0)),
        ],
        out_shape=[
            jax.ShapeDtypeStruct((m, N_COL_BLOCKS * d), BF16),
            jax.ShapeDtypeStruct((m, LANES), F32),
            jax.ShapeDtypeStruct((d, m), BF16),
            vt_shape,
            vt_shape,
        ],
        scratch_shapes=[pltpu.VMEM((tm, d), BF16)],
        compiler_params=pltpu.CompilerParams(
            dimension_semantics=("arbitrary", "arbitrary"), vmem_limit_bytes=VMEM_LIMIT),
        name="in_proj",
    )(x2, g, w_big, w_t, w_small)


def _fcumsum_kernel(s_ref, b_ref, gk_ref, gqt_ref):
    seqlen = s_ref.shape[0]
    blk = SSM_CHUNK
    tri = _lower_tri(blk).astype(BF16)
    slot = lax.broadcasted_iota(jnp.int32, (blk, LANES), 1) % 8
    carry = jnp.zeros((1, LANES), F32)
    for c in range(seqlen // blk):
        rows = slice(c * blk, (c + 1) * blk)
        raw = s_ref[rows, :] + b_ref[...]
        logf = jnp.where(slot == F_LANE, -_softplus(-raw), 0.0)
        f_blk = _tri_cumsum(tri, logf) + carry
        carry = f_blk[blk - 1:blk, :]
        hi, mid, lo = (p.astype(F32) for p in _split3(f_blk * LOG2E))
        gq = hi + pltpu.roll(mid, 1, axis=1) + pltpu.roll(lo, 2, axis=1) + jnp.where((slot >= 3) & (slot < 6), 1.0, 0.0)
        gk = jnp.where(slot < 3, 1.0, 0.0) - (pltpu.roll(hi, 3, axis=1) + pltpu.roll(mid, 4, axis=1)
                                              + pltpu.roll(lo, 5, axis=1))
        gk_ref[rows, :] = gk.astype(BF16)
        gqt_ref[:, rows] = gq.T.astype(BF16)


def _fcumsum(small, fb_vec, bsz, seqlen):
    return pl.pallas_call(
        _fcumsum_kernel,
        grid=(bsz,),
        in_specs=[
            pl.BlockSpec((seqlen, LANES), lambda b: (b, 0)),
            pl.BlockSpec((1, LANES), lambda b: (0, 0)),
        ],
        out_specs=[
            pl.BlockSpec((seqlen, LANES), lambda b: (b, 0)),
            pl.BlockSpec((LANES, seqlen), lambda b: (b, 0)),
        ],
        out_shape=[
            jax.ShapeDtypeStruct((bsz * seqlen, LANES), BF16),
            jax.ShapeDtypeStruct((bsz * LANES, seqlen), BF16),
        ],
        compiler_params=pltpu.CompilerParams(
            dimension_semantics=("arbitrary",), vmem_limit_bytes=VMEM_LIMIT),
        name="fcumsum",
    )(small, fb_vec)


def _attn_kernel(qt_ref, k_ref, gk_ref, vta_ref, vtb_ref, gqt_ref, o_ref,
                 qaug_ref, s_a0, s_a1, s_b0, s_b1, acc_ref, m_ref, *, tq):
    hp = pl.program_id(1)
    i = pl.program_id(2)
    tk = KV_SUB
    sub_per_q = tq // tk
    s_refs = ((s_a0, s_a1), (s_b0, s_b1))
    vt_refs = (vta_ref, vtb_ref)

    grp = lax.broadcasted_iota(jnp.int32, (LANES, tq), 0) // 8
    gq_t = gqt_ref[...].astype(F32)
    zeros_half = jnp.zeros((HEAD_DIM, tq), BF16)
    qaug_ref[0, :HEAD_DIM] = qt_ref[:HEAD_DIM, :]
    qaug_ref[0, HEAD_DIM:LANES] = zeros_half
    qaug_ref[1, :HEAD_DIM] = zeros_half
    qaug_ref[1, HEAD_DIM:LANES] = qt_ref[HEAD_DIM:, :]
    for h in range(2):
        qaug_ref[h, LANES:] = jnp.where(grp == 2 * hp + h, gq_t, 0.0).astype(BF16)
    acc_ref[...] = jnp.zeros(acc_ref.shape, F32)
    m_ref[...] = jnp.full(m_ref.shape, -jnp.inf, F32)

    def qk(sub, n0, slot):
        rows = pl.ds(pl.multiple_of(sub * tk, tk), tk)
        k_sub = jnp.concatenate([k_ref[rows, :], gk_ref[rows, :]], axis=1)
        for h in range(2):
            s_refs[h][slot][:, n0:] = jnp.dot(k_sub, qaug_ref[h, :, n0:], preferred_element_type=F32)

    def update(sub, n0, slot, keep=None):
        for h in range(2):
            s = s_refs[h][slot][:, n0:]
            if keep is not None:
                diag = jnp.where(keep, s[:, :tk], -jnp.inf)
                s = diag if n0 + tk == tq else jnp.concatenate([diag, s[:, tk:]], axis=1)
            m = m_ref[h, 0:1, n0:]
            mn = jnp.maximum(m, jnp.max(s, axis=0, keepdims=True))
            p = jnp.exp2(s - mn).astype(BF16)
            acc_ref[h, :, n0:] = (acc_ref[h, :, n0:] * jnp.exp2(m - mn)
                                  + jnp.dot(vt_refs[h][sub], p, preferred_element_type=F32))
            m_ref[h, 0:1, n0:] = mn

    first_diag = i * sub_per_q
    qk(0, 0, 0)

    def body(j, carry):
        for c in range(sub_per_q):
            sub = j * sub_per_q + c
            qk(sub + 1, 0, (c + 1) % 2)
            update(sub, 0, c % 2)
        return carry

    lax.fori_loop(0, i, body, 0)

    keep = lax.broadcasted_iota(jnp.int32, (tk, tk), 0) <= lax.broadcasted_iota(jnp.int32, (tk, tk), 1)
    for c in range(sub_per_q):
        n0 = c * tk
        if c + 1 < sub_per_q:
            qk(first_diag + c + 1, n0 + tk, (c + 1) % 2)
        update(first_diag + c, n0, c % 2, keep)

    inv_a = 1.0 / acc_ref[0, HEAD_DIM:HEAD_DIM + 1, :]
    inv_b = 1.0 / acc_ref[1, 0:1, :]
    o_t = jnp.concatenate([acc_ref[0, :HEAD_DIM, :] * inv_a, acc_ref[1, HEAD_DIM:, :] * inv_b], axis=0)
    o_ref[...] = o_t.T.astype(BF16)


def _attention(proj, q_t, vta, vtb, gk, gq_t, bsz, seqlen, d_model, tq):
    assert (tq // KV_SUB) % 2 == 0, "score buffers alternate per key sub-tile"
    n_hp = d_model // LANES
    per_row = d_model // LANES
    nq = seqlen // tq
    n_sub = seqlen // KV_SUB
    score_buf = pltpu.VMEM((KV_SUB, tq), F32)
    return pl.pallas_call(
        functools.partial(_attn_kernel, tq=tq),
        grid=(bsz, n_hp, nq),
        in_specs=[
            pl.BlockSpec((LANES, tq), lambda b, h, i: (h, b * nq + i)),
            pl.BlockSpec((seqlen, LANES), lambda b, h, i: (b, COL_K * per_row + h)),
            pl.BlockSpec((seqlen, LANES), lambda b, h, i: (b, 0)),
            pl.BlockSpec((n_sub, LANES, KV_SUB), lambda b, h, i: (b, h, 0)),
            pl.BlockSpec((n_sub, LANES, KV_SUB), lambda b, h, i: (b, h, 0)),
            pl.BlockSpec((LANES, tq), lambda b, h, i: (b, i)),
        ],
        out_specs=pl.BlockSpec((tq, LANES), lambda b, h, i: (b * nq + i, h)),
        out_shape=jax.ShapeDtypeStruct((bsz * seqlen, d_model), BF16),
        scratch_shapes=[
            pltpu.VMEM((2, 2 * LANES, tq), BF16),
            score_buf, score_buf, score_buf, score_buf,
            pltpu.VMEM((2, LANES, tq), F32),
            pltpu.VMEM((2, 8, tq), F32),
        ],
        compiler_params=pltpu.CompilerParams(
            dimension_semantics=("arbitrary", "arbitrary", "arbitrary"), vmem_limit_bytes=VMEM_LIMIT),
        name="fox_attention",
    )(q_t, proj, gk, vta, vtb, gq_t)


def _ssd_chunk(is_first, xs_ref, bc_ref, s_ref, cw_ref, cb_ref, dtb_ref, alog_ref, dskip_ref,
               expand_ref, shift_ref, y_ref, prev_ref, st_ref, n_heads):
    q = SSM_CHUNK
    d_inner = xs_ref.shape[1]
    gn = SSM_GROUPS * SSM_STATE
    heads_per_group = n_heads // SSM_GROUPS
    gw = heads_per_group * HEAD_DIM

    if is_first is not None:
        @pl.when(is_first)
        def _():
            prev_ref[...] = jnp.zeros(prev_ref.shape, BF16)
            st_ref[...] = jnp.zeros(st_ref.shape, F32)

    cur = jnp.concatenate([xs_ref[...], bc_ref[...]], axis=1)
    both = jnp.concatenate([cur, prev_ref[...]], axis=0)
    prev_ref[...] = cur
    acts = []
    for n in range(cur.shape[1] // CONV_COLS):
        cols = slice(n * CONV_COLS, (n + 1) * CONV_COLS)
        shifted = jnp.dot(shift_ref[...], both[:, cols], preferred_element_type=F32)
        conv = cb_ref[:, cols] + cw_ref[SSM_CONV - 1:SSM_CONV, cols] * cur[:, cols].astype(F32)
        for jj in range(SSM_CONV - 1):
            k = SSM_CONV - 1 - jj
            conv = conv + cw_ref[jj:jj + 1, cols] * shifted[(k - 1) * q:k * q, :]
        acts.append(_silu(conv))
        yield
    act = jnp.concatenate(acts, axis=1)
    xs = act[:, :d_inner]
    b16 = act[:, d_inner:d_inner + gn].astype(BF16)
    c16 = act[:, d_inner + gn:].astype(BF16)

    lane = lax.broadcasted_iota(jnp.int32, (q, LANES), 1)
    head_lane = lane % 8 == DT_LANE
    dt = jnp.where(head_lane, _softplus(s_ref[...] + dtb_ref[...]), 0.0)
    da = dt * (-jnp.exp(alog_ref[...]))
    tri = _lower_tri(q)
    cs = _tri_cumsum(tri.astype(BF16), da)
    yield
    cs2 = cs * LOG2E
    key_t = (cs2 - jnp.log2(dt)).T
    last = cs[q - 1:q, :]
    w_end = dt * jnp.exp(last - cs)
    ecs = jnp.exp(cs)
    yield

    def expand(v):
        hi, mid, _ = _split3(jnp.where(head_lane, v, 0.0))
        both = hi.astype(F32) + pltpu.roll(mid.astype(F32), 1, axis=1)
        return jnp.dot(both.astype(BF16), expand_ref[...], preferred_element_type=F32)

    w_e = expand(w_end)
    ecs_e = expand(ecs)
    xw16 = (xs * w_e).astype(BF16)
    yield

    xs16 = xs.astype(BF16)
    first_half = lax.broadcasted_iota(jnp.int32, (q, LANES), 1) < HEAD_DIM

    y_parts = []
    for g in range(SSM_GROUPS):
        bg = b16[:, g * SSM_STATE:(g + 1) * SSM_STATE]
        cg = c16[:, g * SSM_STATE:(g + 1) * SSM_STATE]
        cb = lax.dot_general(cg, bg, (((1,), (1,)), ((), ())), preferred_element_type=F32)
        st_prev = st_ref[g]
        y_off = jnp.dot(cg, st_prev.astype(BF16), preferred_element_type=F32)
        yield
        for pair in range(heads_per_group // 2):
            lo_lane = g * gw + pair * LANES
            halves = []
            for half in range(2):
                h = 8 * (g * heads_per_group + pair * 2 + half) + DT_LANE
                seg = cs2[:, h:h + 1] - key_t[h:h + 1, :]
                mmat = (cb * jnp.exp2(jnp.where(tri, seg, -jnp.inf))).astype(BF16)
                halves.append(jnp.dot(mmat, xs16[:, lo_lane:lo_lane + LANES], preferred_element_type=F32))
            y_pair = jnp.where(first_half, halves[0], halves[1])
            y_pair = y_pair + y_off[:, pair * LANES:(pair + 1) * LANES] * ecs_e[:, lo_lane:lo_lane + LANES]
            y_parts.append(y_pair)
            yield
        bg_t = bg.astype(F32).T.astype(BF16)
        new_st = jnp.dot(bg_t, xw16[:, g * gw:(g + 1) * gw], preferred_element_type=F32)
        st_ref[g] = st_prev * ecs_e[q - 1:q, g * gw:(g + 1) * gw] + new_st
        yield

    y_ref[...] = (jnp.concatenate(y_parts, axis=1) + xs * dskip_ref[...]).astype(BF16)
    yield


def _ssd_kernel(xs_ref, bc_ref, s_ref, cw_ref, cb_ref, dtb_ref, alog_ref, dskip_ref,
                expand_ref, shift_ref, y_ref, prev_ref, st_ref, *, n_heads):
    c = pl.program_id(1)
    q = SSM_CHUNK
    for cc in range(xs_ref.shape[1] // q):
        rows = slice(cc * q, (cc + 1) * q)
        is_first = (c == 0) if cc == 0 else None
        chunks = [_ssd_chunk(is_first, xs_ref.at[seq, rows], bc_ref.at[seq, rows], s_ref.at[seq, rows], cw_ref, cb_ref,
                             dtb_ref, alog_ref, dskip_ref, expand_ref, shift_ref, y_ref.at[seq, rows],
                             prev_ref.at[seq], st_ref.at[seq], n_heads)
                  for seq in range(xs_ref.shape[0])]
        for _ in zip(*chunks):
            pass


def _ssd(proj, small, conv_w, conv_b, dtb_vec, alog_vec, dskip_e, expand_m, shift_m, bsz, seqlen, d_model, n_heads):
    q = SSM_CHUNK
    conv_dim = conv_w.shape[1]
    gw = (n_heads // SSM_GROUPS) * HEAD_DIM
    nb = 2 if bsz % 2 == 0 else 1
    rows = SSD_CHUNKS_PER_STEP * q
    assert seqlen % rows == 0
    proj3 = proj.reshape(bsz, seqlen, proj.shape[1])
    const = lambda b, c: (0, 0)
    y = pl.pallas_call(
        functools.partial(_ssd_kernel, n_heads=n_heads),
        grid=(bsz // nb, seqlen // rows),
        in_specs=[
            pl.BlockSpec((nb, rows, d_model), lambda b, c: (b, c, COL_XS)),
            pl.BlockSpec((nb, rows, d_model), lambda b, c: (b, c, COL_BC)),
            pl.BlockSpec((nb, rows, LANES), lambda b, c: (b, c, 0)),
            pl.BlockSpec((SSM_CONV, conv_dim), const),
            pl.BlockSpec((1, conv_dim), const),
            pl.BlockSpec((1, LANES), const),
            pl.BlockSpec((1, LANES), const),
            pl.BlockSpec((1, d_model), const),
            pl.BlockSpec((LANES, d_model), const),
            pl.BlockSpec(((SSM_CONV - 1) * q, 2 * q), const),
        ],
        out_specs=pl.BlockSpec((nb, rows, d_model), lambda b, c: (b, c, 0)),
        out_shape=jax.ShapeDtypeStruct((bsz, seqlen, d_model), BF16),
        scratch_shapes=[
            pltpu.VMEM((nb, q, conv_dim), BF16),
            pltpu.VMEM((nb, SSM_GROUPS, SSM_STATE, gw), F32),
        ],
        compiler_params=pltpu.CompilerParams(
            dimension_semantics=("arbitrary", "arbitrary"), vmem_limit_bytes=VMEM_LIMIT),
        name="ssd_branch",
    )(proj3, proj3, small.reshape(bsz, seqlen, LANES), conv_w, conv_b, dtb_vec, alog_vec, dskip_e, expand_m, shift_m)
    return y.reshape(bsz * seqlen, d_model)


def _half_silu_gate(v16, zh16):
    return v16 * zh16 * (jnp.tanh(zh16) + 1.0)


def _merge_kernel(ys_ref, zs_ref, oa_ref, za_ref, gs_ref, ga_ref, x_ref, ws_ref, wa_ref, wo_ref, pg_ref, o_ref):
    y_att = _half_silu_gate(oa_ref[...], za_ref[...])
    p_att = jnp.dot(y_att, wa_ref[...], preferred_element_type=F32)
    yg = _half_silu_gate(ys_ref[...], zs_ref[...]).astype(F32)
    gw = yg.shape[1] // SSM_GROUPS
    normed = []
    for g in range(SSM_GROUPS):
        blk = yg[:, g * gw:(g + 1) * gw]
        normed.append(blk * lax.rsqrt(jnp.mean(blk * blk, axis=-1, keepdims=True) + RMS_EPS))
    y_ssm = jnp.concatenate(normed, axis=1).astype(BF16)
    p_ssm = jnp.dot(y_ssm, ws_ref[...], preferred_element_type=F32)
    gate_s = (jnp.tanh(gs_ref[...]) + 1.0).astype(F32)
    gate_a = (jnp.tanh(ga_ref[...]) + 1.0).astype(F32)
    merged = gate_s * p_ssm + gate_a * p_att
    out = jnp.dot(merged.astype(BF16), wo_ref[...], preferred_element_type=F32)
    ms = jnp.mean(out * out, axis=-1, keepdims=True)
    o_ref[...] = x_ref[...] + out * lax.rsqrt(ms + RMS_EPS) * pg_ref[...]


def _merge(y_scan, o_att, proj, x2, w_ssm, w_att, w_out, post_g, tm):
    m, d = x2.shape
    row = lambda i: (i, 0)
    full = lambda i: (0, 0)
    return pl.pallas_call(
        _merge_kernel,
        grid=(m // tm,),
        in_specs=[
            pl.BlockSpec((tm, d), row),
            pl.BlockSpec((tm, d), lambda i: (i, COL_Z_SSM)),
            pl.BlockSpec((tm, d), row),
            pl.BlockSpec((tm, d), lambda i: (i, COL_Z_ATT)),
            pl.BlockSpec((tm, d), lambda i: (i, COL_G_SSM)),
            pl.BlockSpec((tm, d), lambda i: (i, COL_G_ATT)),
            pl.BlockSpec((tm, d), row),
            pl.BlockSpec((d, d), full),
            pl.BlockSpec((d, d), full),
            pl.BlockSpec((d, d), full),
            pl.BlockSpec((1, d), full),
        ],
        out_specs=pl.BlockSpec((tm, d), row),
        out_shape=jax.ShapeDtypeStruct((m, d), F32),
        compiler_params=pltpu.CompilerParams(
            dimension_semantics=("arbitrary",), vmem_limit_bytes=VMEM_LIMIT),
        name="gated_merge",
    )(y_scan, proj, o_att, proj, proj, proj, x2, w_ssm, w_att, w_out, post_g)


def _head_lanes(v, slot):
    cols = [jnp.zeros_like(v, dtype=F32)] * 8
    cols[slot] = v.astype(F32)
    return jnp.stack(cols, axis=-1).reshape(1, LANES)


def _layer(x, pre_g, w_in, conv_w, conv_b, dt_bias, a_log, d_skip, ssm_norm_g, fgate_b,
           w_branch_ssm, w_branch_att, w_out, post_g):
    bsz, seqlen, d = x.shape
    n_heads = d // HEAD_DIM
    assert n_heads * 8 == LANES, "small-projection lane layout assumes 16 heads"
    gn = SSM_GROUPS * SSM_STATE
    sizes = (d, d + 2 * gn, n_heads, d, d, d, n_heads, d, d, d)
    offs = [0]
    for sz in sizes:
        offs.append(offs[-1] + sz)
    z_ssm, xbc, dt_w, q_w, k_w, v_w, f_w, z_att, g_ssm, g_att = (w_in[:, offs[n]:offs[n + 1]] for n in range(10))
    w_big = jnp.concatenate([0.5 * z_ssm, xbc, k_w, 0.5 * z_att, 0.5 * g_ssm, 0.5 * g_att], axis=1).astype(BF16)
    w_t = jnp.concatenate([(q_w * (HEAD_DIM ** -0.5 * LOG2E)).T, v_w.T], axis=0).astype(BF16)
    slots = [jnp.zeros_like(f_w)] * 8
    slots[F_LANE], slots[DT_LANE] = f_w, dt_w
    w_small = jnp.stack(slots, axis=-1).reshape(d, LANES).astype(BF16)

    x2 = x.reshape(bsz * seqlen, d)
    proj, small, q_t, vta, vtb = _in_proj(x2, pre_g.reshape(1, d), w_big, w_t, w_small, tm=min(1024, bsz * seqlen))

    gk, gq_t = _fcumsum(small, _head_lanes(fgate_b, F_LANE), bsz, seqlen)
    o_att = _attention(proj, q_t, vta, vtb, gk, gq_t, bsz, seqlen, d, tq=min(2048, seqlen))

    head_of_col = np.arange(d) // HEAD_DIM
    lane_ids = np.arange(LANES)[:, None]
    expand_m = jnp.asarray((lane_ids == 8 * head_of_col[None, :] + DT_LANE)
                           | (lane_ids == 8 * head_of_col[None, :] + DT_LANE + 1), BF16)
    t_ids = np.arange(SSM_CHUNK)[:, None]
    src = np.arange(2 * SSM_CHUNK)[None, :]
    shift_m = jnp.asarray(np.concatenate([src == np.where(t_ids >= k, t_ids - k, 2 * SSM_CHUNK + t_ids - k)
                                          for k in range(1, SSM_CONV)], axis=0), BF16)
    y_scan = _ssd(proj, small, conv_w, conv_b.reshape(1, -1), _head_lanes(dt_bias, DT_LANE), _head_lanes(a_log, DT_LANE),
                 jnp.repeat(d_skip.astype(F32), HEAD_DIM).reshape(1, d), expand_m, shift_m,
                 bsz, seqlen, d, n_heads)

    out = _merge(y_scan, o_att, proj, x2, (ssm_norm_g.astype(F32)[:, None] * w_branch_ssm).astype(BF16),
                 w_branch_att.astype(BF16), (0.5 * w_out).astype(BF16), post_g.reshape(1, d), tm=min(512, bsz * seqlen))
    return out.reshape(bsz, seqlen, d)


def kernel(x, pre_norm_g, w_in, conv_w, conv_b, dt_bias, a_log, d_skip, ssm_norm_g, fgate_b,
           w_branch_ssm, w_branch_att, w_out, post_norm_g):
    for i in range(pre_norm_g.shape[0]):
        x = _layer(x, pre_norm_g[i], w_in[i], conv_w[i], conv_b[i], dt_bias[i], a_log[i], d_skip[i],
                   ssm_norm_g[i], fgate_b[i], w_branch_ssm[i], w_branch_att[i], w_out[i], post_norm_g[i])
    return x
```

```python
import functools

import jax
import jax.numpy as jnp
import numpy as np
from jax import lax
from jax.experimental import pallas as pl
from jax.experimental.pallas import tpu as pltpu

F32 = jnp.float32
BF16 = jnp.bfloat16

RMS_EPS = 1e-6
HEAD_DIM = 64
SSM_GROUPS = 4
SSM_STATE = 128
SSM_CONV = 4
SSM_CHUNK = 128
CONV_COLS = 512
SSD_CHUNKS_PER_STEP = 4
NORM_ROWS = 256
KV_SUB = 256
LANES = 128
VMEM_LIMIT = 48 * 1024 * 1024

COL_Z_SSM, COL_XS, COL_BC, COL_K, COL_Z_ATT, COL_G_SSM, COL_G_ATT = range(7)
N_COL_BLOCKS = 7
PROJ_STEPS = 4
N_T_BLOCKS = 2
LOG2E = 1.4426950408889634
F_LANE = 0
DT_LANE = 4


def _sigmoid(x):
    return 0.5 * jnp.tanh(0.5 * x) + 0.5


def _silu(x):
    return x * _sigmoid(x)


def _softplus(x):
    return jnp.maximum(x, 0.0) + jnp.log(1.0 + jnp.exp(-jnp.abs(x)))


def _split3(x):
    hi = x.astype(BF16)
    r1 = x - hi.astype(F32)
    mid = r1.astype(BF16)
    lo = (r1 - mid.astype(F32)).astype(BF16)
    return hi, mid, lo


def _tri_cumsum(tri_bf16, x):
    hi, mid, lo = _split3(x)
    dot = functools.partial(jnp.dot, preferred_element_type=F32)
    return dot(tri_bf16, hi) + dot(tri_bf16, mid) + dot(tri_bf16, lo)


def _lower_tri(n):
    r = lax.broadcasted_iota(jnp.int32, (n, n), 0)
    c = lax.broadcasted_iota(jnp.int32, (n, n), 1)
    return r >= c


def _in_proj_kernel(x_ref, g_ref, w_ref, wt_ref, ws_ref, p_ref, s_ref, qt_ref, vta_ref, vtb_ref, h_ref):
    j = pl.program_id(1)
    nt = (((1,), (1,)), ((), ()))

    @pl.when(j == 0)
    def _():
        for r0 in range(0, x_ref.shape[0], NORM_ROWS):
            rows = slice(r0, r0 + NORM_ROWS)
            x = x_ref[rows, :]
            ms = jnp.mean(x * x, axis=-1, keepdims=True)
            h = (x * lax.rsqrt(ms + RMS_EPS) * g_ref[...]).astype(BF16)
            h_ref[rows, :] = h
            s_ref[rows, :] = jnp.dot(h, ws_ref[...], preferred_element_type=F32)
            p_ref[rows, :] = jnp.dot(h, w_ref[...], preferred_element_type=F32).astype(BF16)

    @pl.when((j > 0) & (j < PROJ_STEPS))
    def _():
        p_ref[...] = jnp.dot(h_ref[...], w_ref[...], preferred_element_type=F32).astype(BF16)

    @pl.when(j == PROJ_STEPS)
    def _():
        d = qt_ref.shape[0]
        qt_ref[...] = lax.dot_general(wt_ref[:d, :], h_ref[...], nt, preferred_element_type=F32).astype(BF16)
        vt = lax.dot_general(wt_ref[d:, :], h_ref[...], nt, preferred_element_type=F32)
        first = (lax.broadcasted_iota(jnp.int32, vt.shape, 0) % LANES) < HEAD_DIM
        vta = jnp.where(first, vt, 1.0).astype(BF16)
        vtb = jnp.where(first, 1.0, vt).astype(BF16)
        for c in range(vta_ref.shape[0]):
            vta_ref[c] = vta[:, c * KV_SUB:(c + 1) * KV_SUB]
            vtb_ref[c] = vtb[:, c * KV_SUB:(c + 1) * KV_SUB]


def _in_proj(x2, g, w_big, w_t, w_small, tm):
    m, d = x2.shape
    last = PROJ_STEPS - 1
    wide = N_COL_BLOCKS * d // PROJ_STEPS
    assert wide % LANES == 0 and wide * PROJ_STEPS == N_COL_BLOCKS * d
    sub = tm // KV_SUB
    vt_shape = jax.ShapeDtypeStruct((m // KV_SUB, d, KV_SUB), BF16)
    return pl.pallas_call(
        _in_proj_kernel,
        grid=(m // tm, PROJ_STEPS + 1),
        in_specs=[
            pl.BlockSpec((tm, d), lambda i, j: (i, 0)),
            pl.BlockSpec((1, d), lambda i, j: (0, 0)),
            pl.BlockSpec((d, wide), lambda i, j: (0, jnp.minimum(j, last))),
            pl.BlockSpec((N_T_BLOCKS * d, d), lambda i, j: (0, 0)),
            pl.BlockSpec((d, LANES), lambda i, j: (0, 0)),
        ],
        out_specs=[
            pl.BlockSpec((tm, wide), lambda i, j: (i, jnp.minimum(j, last))),
            pl.BlockSpec((tm, LANES), lambda i, j: (i, 0)),
            pl.BlockSpec((d, tm), lambda i, j: (0, i)),
            pl.BlockSpec((sub, d, KV_SUB), lambda i, j: (i, 0, 0)),
            pl.BlockSpec((sub, d, KV_SUB), lambda i, j: (i, 0, 0)),
        ],
        out_shape=[
            jax.ShapeDtypeStruct((m, N_COL_BLOCKS * d), BF16),
            jax.ShapeDtypeStruct((m, LANES), F32),
            jax.ShapeDtypeStruct((d, m), BF16),
            vt_shape,
            vt_shape,
        ],
        scratch_shapes=[pltpu.VMEM((tm, d), BF16)],
        compiler_params=pltpu.CompilerParams(
            dimension_semantics=("arbitrary", "arbitrary"), vmem_limit_bytes=VMEM_LIMIT),
        name="in_proj",
    )(x2, g, w_big, w_t, w_small)


def _fcumsum_kernel(s_ref, b_ref, gk_ref, gqt_ref):
    seqlen = s_ref.shape[0]
    blk = SSM_CHUNK
    tri = _lower_tri(blk).astype(BF16)
    slot = lax.broadcasted_iota(jnp.int32, (blk, LANES), 1) % 8
    carry = jnp.zeros((1, LANES), F32)
    for c in range(seqlen // blk):
        rows = slice(c * blk, (c + 1) * blk)
        raw = s_ref[rows, :] + b_ref[...]
        logf = jnp.where(slot == F_LANE, -_softplus(-raw), 0.0)
        f_blk = _tri_cumsum(tri, logf) + carry
        carry = f_blk[blk - 1:blk, :]
        hi, mid, lo = (p.astype(F32) for p in _split3(f_blk * LOG2E))
        gq = hi + pltpu.roll(mid, 1, axis=1) + pltpu.roll(lo, 2, axis=1) + jnp.where((slot >= 3) & (slot < 6), 1.0, 0.0)
        gk = jnp.where(slot < 3, 1.0, 0.0) - (pltpu.roll(hi, 3, axis=1) + pltpu.roll(mid, 4, axis=1)
                                              + pltpu.roll(lo, 5, axis=1))
        gk_ref[rows, :] = gk.astype(BF16)
        gqt_ref[:, rows] = gq.T.astype(BF16)


def _fcumsum(small, fb_vec, bsz, seqlen):
    return pl.pallas_call(
        _fcumsum_kernel,
        grid=(bsz,),
        in_specs=[
            pl.BlockSpec((seqlen, LANES), lambda b: (b, 0)),
            pl.BlockSpec((1, LANES), lambda b: (0, 0)),
        ],
        out_specs=[
            pl.BlockSpec((seqlen, LANES), lambda b: (b, 0)),
            pl.BlockSpec((LANES, seqlen), lambda b: (b, 0)),
        ],
        out_shape=[
            jax.ShapeDtypeStruct((bsz * seqlen, LANES), BF16),
            jax.ShapeDtypeStruct((bsz * LANES, seqlen), BF16),
        ],
        compiler_params=pltpu.CompilerParams(
            dimension_semantics=("arbitrary",), vmem_limit_bytes=VMEM_LIMIT),
        name="fcumsum",
    )(small, fb_vec)


def _attn_kernel(qt_ref, k_ref, gk_ref, vta_ref, vtb_ref, gqt_ref, o_ref,
                 qaug_ref, s_a0, s_a1, s_b0, s_b1, acc_ref, m_ref, *, tq):
    hp = pl.program_id(1)
    i = pl.program_id(2)
    tk = KV_SUB
    sub_per_q = tq // tk
    s_refs = ((s_a0, s_a1), (s_b0, s_b1))
    vt_refs = (vta_ref, vtb_ref)

    grp = lax.broadcasted_iota(jnp.int32, (LANES, tq), 0) // 8
    gq_t = gqt_ref[...].astype(F32)
    zeros_half = jnp.zeros((HEAD_DIM, tq), BF16)
    qaug_ref[0, :HEAD_DIM] = qt_ref[:HEAD_DIM, :]
    qaug_ref[0, HEAD_DIM:LANES] = zeros_half
    qaug_ref[1, :HEAD_DIM] = zeros_half
    qaug_ref[1, HEAD_DIM:LANES] = qt_ref[HEAD_DIM:, :]
    for h in range(2):
        qaug_ref[h, LANES:] = jnp.where(grp == 2 * hp + h, gq_t, 0.0).astype(BF16)
    acc_ref[...] = jnp.zeros(acc_ref.shape, F32)
    m_ref[...] = jnp.full(m_ref.shape, -jnp.inf, F32)

    def qk(sub, n0, slot):
        rows = pl.ds(pl.multiple_of(sub * tk, tk), tk)
        k_sub = jnp.concatenate([k_ref[rows, :], gk_ref[rows, :]], axis=1)
        for h in range(2):
            s_refs[h][slot][:, n0:] = jnp.dot(k_sub, qaug_ref[h, :, n0:], preferred_element_type=F32)

    def update(sub, n0, slot, keep=None):
        for h in range(2):
            s = s_refs[h][slot][:, n0:]
            if keep is not None:
                diag = jnp.where(keep, s[:, :tk], -jnp.inf)
                s = diag if n0 + tk == tq else jnp.concatenate([diag, s[:, tk:]], axis=1)
            m = m_ref[h, 0:1, n0:]
            mn = jnp.maximum(m, jnp.max(s, axis=0, keepdims=True))
            p = jnp.exp2(s - mn).astype(BF16)
            acc_ref[h, :, n0:] = (acc_ref[h, :, n0:] * jnp.exp2(m - mn)
                                  + jnp.dot(vt_refs[h][sub], p, preferred_element_type=F32))
            m_ref[h, 0:1, n0:] = mn

    first_diag = i * sub_per_q
    qk(0, 0, 0)

    def body(j, carry):
        for c in range(sub_per_q):
            sub = j * sub_per_q + c
            qk(sub + 1, 0, (c + 1) % 2)
            update(sub, 0, c % 2)
        return carry

    lax.fori_loop(0, i, body, 0)

    keep = lax.broadcasted_iota(jnp.int32, (tk, tk), 0) <= lax.broadcasted_iota(jnp.int32, (tk, tk), 1)
    for c in range(sub_per_q):
        n0 = c * tk
        if c + 1 < sub_per_q:
            qk(first_diag + c + 1, n0 + tk, (c + 1) % 2)
        update(first_diag + c, n0, c % 2, keep)

    inv_a = 1.0 / acc_ref[0, HEAD_DIM:HEAD_DIM + 1, :]
    inv_b = 1.0 / acc_ref[1, 0:1, :]
    o_t = jnp.concatenate([acc_ref[0, :HEAD_DIM, :] * inv_a, acc_ref[1, HEAD_DIM:, :] * inv_b], axis=0)
    o_ref[...] = o_t.T.astype(BF16)


def _attention(proj, q_t, vta, vtb, gk, gq_t, bsz, seqlen, d_model, tq):
    assert (tq // KV_SUB) % 2 == 0, "score buffers alternate per key sub-tile"
    n_hp = d_model // LANES
    per_row = d_model // LANES
    nq = seqlen // tq
    n_sub = seqlen // KV_SUB
    score_buf = pltpu.VMEM((KV_SUB, tq), F32)
    return pl.pallas_call(
        functools.partial(_attn_kernel, tq=tq),
        grid=(bsz, n_hp, nq),
        in_specs=[
            pl.BlockSpec((LANES, tq), lambda b, h, i: (h, b * nq + i)),
            pl.BlockSpec((seqlen, LANES), lambda b, h, i: (b, COL_K * per_row + h)),
            pl.BlockSpec((seqlen, LANES), lambda b, h, i: (b, 0)),
            pl.BlockSpec((n_sub, LANES, KV_SUB), lambda b, h, i: (b, h, 0)),
            pl.BlockSpec((n_sub, LANES, KV_SUB), lambda b, h, i: (b, h, 0)),
            pl.BlockSpec((LANES, tq), lambda b, h, i: (b, i)),
        ],
        out_specs=pl.BlockSpec((tq, LANES), lambda b, h, i: (b * nq + i, h)),
        out_shape=jax.ShapeDtypeStruct((bsz * seqlen, d_model), BF16),
        scratch_shapes=[
            pltpu.VMEM((2, 2 * LANES, tq), BF16),
            score_buf, score_buf, score_buf, score_buf,
            pltpu.VMEM((2, LANES, tq), F32),
            pltpu.VMEM((2, 8, tq), F32),
        ],
        compiler_params=pltpu.CompilerParams(
            dimension_semantics=("arbitrary", "arbitrary", "arbitrary"), vmem_limit_bytes=VMEM_LIMIT),
        name="fox_attention",
    )(q_t, proj, gk, vta, vtb, gq_t)


def _ssd_chunk(is_first, xs_ref, bc_ref, s_ref, cw_ref, cb_ref, dtb_ref, alog_ref, dskip_ref,
               expand_ref, shift_ref, y_ref, prev_ref, st_ref, n_heads):
    q = SSM_CHUNK
    d_inner = xs_ref.shape[1]
    gn = SSM_GROUPS * SSM_STATE
    heads_per_group = n_heads // SSM_GROUPS
    gw = heads_per_group * HEAD_DIM

    if is_first is not None:
        @pl.when(is_first)
        def _():
            prev_ref[...] = jnp.zeros(prev_ref.shape, BF16)
            st_ref[...] = jnp.zeros(st_ref.shape, F32)

    cur = jnp.concatenate([xs_ref[...], bc_ref[...]], axis=1)
    both = jnp.concatenate([cur, prev_ref[...]], axis=0)
    prev_ref[...] = cur
    acts = []
    for n in range(cur.shape[1] // CONV_COLS):
        cols = slice(n * CONV_COLS, (n + 1) * CONV_COLS)
        shifted = jnp.dot(shift_ref[...], both[:, cols], preferred_element_type=F32)
        conv = cb_ref[:, cols] + cw_ref[SSM_CONV - 1:SSM_CONV, cols] * cur[:, cols].astype(F32)
        for jj in range(SSM_CONV - 1):
            k = SSM_CONV - 1 - jj
            conv = conv + cw_ref[jj:jj + 1, cols] * shifted[(k - 1) * q:k * q, :]
        acts.append(_silu(conv))
        yield
    act = jnp.concatenate(acts, axis=1)
    xs = act[:, :d_inner]
    b16 = act[:, d_inner:d_inner + gn].astype(BF16)
    c16 = act[:, d_inner + gn:].astype(BF16)

    lane = lax.broadcasted_iota(jnp.int32, (q, LANES), 1)
    head_lane = lane % 8 == DT_LANE
    dt = jnp.where(head_lane, _softplus(s_ref[...] + dtb_ref[...]), 0.0)
    da = dt * (-jnp.exp(alog_ref[...]))
    tri = _lower_tri(q)
    cs = _tri_cumsum(tri.astype(BF16), da)
    yield
    cs2 = cs * LOG2E
    key_t = (cs2 - jnp.log2(dt)).T
    last = cs[q - 1:q, :]
    w_end = dt * jnp.exp(last - cs)
    ecs = jnp.exp(cs)
    yield

    def expand(v):
        hi, mid, _ = _split3(jnp.where(head_lane, v, 0.0))
        both = hi.astype(F32) + pltpu.roll(mid.astype(F32), 1, axis=1)
        return jnp.dot(both.astype(BF16), expand_ref[...], preferred_element_type=F32)

    w_e = expand(w_end)
    ecs_e = expand(ecs)
    xw16 = (xs * w_e).astype(BF16)
    yield

    xs16 = xs.astype(BF16)
    first_half = lax.broadcasted_iota(jnp.int32, (q, LANES), 1) < HEAD_DIM

    y_parts = []
    for g in range(SSM_GROUPS):
        bg = b16[:, g * SSM_STATE:(g + 1) * SSM_STATE]
        cg = c16[:, g * SSM_STATE:(g + 1) * SSM_STATE]
        cb = lax.dot_general(cg, bg, (((1,), (1,)), ((), ())), preferred_element_type=F32)
        st_prev = st_ref[g]
        y_off = jnp.dot(cg, st_prev.astype(BF16), preferred_element_type=F32)
        yield
        for pair in range(heads_per_group // 2):
            lo_lane = g * gw + pair * LANES
            halves = []
            for half in range(2):
                h = 8 * (g * heads_per_group + pair * 2 + half) + DT_LANE
                seg = cs2[:, h:h + 1] - key_t[h:h + 1, :]
                mmat = (cb * jnp.exp2(jnp.where(tri, seg, -jnp.inf))).astype(BF16)
                halves.append(jnp.dot(mmat, xs16[:, lo_lane:lo_lane + LANES], preferred_element_type=F32))
            y_pair = jnp.where(first_half, halves[0], halves[1])
            y_pair = y_pair + y_off[:, pair * LANES:(pair + 1) * LANES] * ecs_e[:, lo_lane:lo_lane + LANES]
            y_parts.append(y_pair)
            yield
        bg_t = bg.astype(F32).T.astype(BF16)
        new_st = jnp.dot(bg_t, xw16[:, g * gw:(g + 1) * gw], preferred_element_type=F32)
        st_ref[g] = st_prev * ecs_e[q - 1:q, g * gw:(g + 1) * gw] + new_st
        yield

    y_ref[...] = (jnp.concatenate(y_parts, axis=1) + xs * dskip_ref[...]).astype(BF16)
    yield


def _ssd_kernel(xs_ref, bc_ref, s_ref, cw_ref, cb_ref, dtb_ref, alog_ref, dskip_ref,
                expand_ref, shift_ref, y_ref, prev_ref, st_ref, *, n_heads):
    c = pl.program_id(1)
    q = SSM_CHUNK
    for cc in range(xs_ref.shape[1] // q):
        rows = slice(cc * q, (cc + 1) * q)
        is_first = (c == 0) if cc == 0 else None
        chunks = [_ssd_chunk(is_first, xs_ref.at[seq, rows], bc_ref.at[seq, rows], s_ref.at[seq, rows], cw_ref, cb_ref,
                             dtb_ref, alog_ref, dskip_ref, expand_ref, shift_ref, y_ref.at[seq, rows],
                             prev_ref.at[seq], st_ref.at[seq], n_heads)
                  for seq in range(xs_ref.shape[0])]
        for _ in zip(*chunks):
            pass


def _ssd(proj, small, conv_w, conv_b, dtb_vec, alog_vec, dskip_e, expand_m, shift_m, bsz, seqlen, d_model, n_heads):
    q = SSM_CHUNK
    conv_dim = conv_w.shape[1]
    gw = (n_heads // SSM_GROUPS) * HEAD_DIM
    nb = 2 if bsz % 2 == 0 else 1
    rows = SSD_CHUNKS_PER_STEP * q
    assert seqlen % rows == 0
    proj3 = proj.reshape(bsz, seqlen, proj.shape[1])
    const = lambda b, c: (0, 0)
    y = pl.pallas_call(
        functools.partial(_ssd_kernel, n_heads=n_heads),
        grid=(bsz // nb, seqlen // rows),
        in_specs=[
            pl.BlockSpec((nb, rows, d_model), lambda b, c: (b, c, COL_XS)),
            pl.BlockSpec((nb, rows, d_model), lambda b, c: (b, c, COL_BC)),
            pl.BlockSpec((nb, rows, LANES), lambda b, c: (b, c, 0)),
            pl.BlockSpec((SSM_CONV, conv_dim), const),
            pl.BlockSpec((1, conv_dim), const),
            pl.BlockSpec((1, LANES), const),
            pl.BlockSpec((1, LANES), const),
            pl.BlockSpec((1, d_model), const),
            pl.BlockSpec((LANES, d_model), const),
            pl.BlockSpec(((SSM_CONV - 1) * q, 2 * q), const),
        ],
        out_specs=pl.BlockSpec((nb, rows, d_model), lambda b, c: (b, c, 0)),
        out_shape=jax.ShapeDtypeStruct((bsz, seqlen, d_model), BF16),
        scratch_shapes=[
            pltpu.VMEM((nb, q, conv_dim), BF16),
            pltpu.VMEM((nb, SSM_GROUPS, SSM_STATE, gw), F32),
        ],
        compiler_params=pltpu.CompilerParams(
            dimension_semantics=("arbitrary", "arbitrary"), vmem_limit_bytes=VMEM_LIMIT),
        name="ssd_branch",
    )(proj3, proj3, small.reshape(bsz, seqlen, LANES), conv_w, conv_b, dtb_vec, alog_vec, dskip_e, expand_m, shift_m)
    return y.reshape(bsz * seqlen, d_model)


def _half_silu_gate(v16, zh16):
    return v16 * zh16 * (jnp.tanh(zh16) + 1.0)


def _merge_kernel(ys_ref, zs_ref, oa_ref, za_ref, gs_ref, ga_ref, x_ref, ws_ref, wa_ref, wo_ref, pg_ref, o_ref):
    y_att = _half_silu_gate(oa_ref[...], za_ref[...])
    p_att = jnp.dot(y_att, wa_ref[...], preferred_element_type=F32)
    yg = _half_silu_gate(ys_ref[...], zs_ref[...]).astype(F32)
    gw = yg.shape[1] // SSM_GROUPS
    normed = []
    for g in range(SSM_GROUPS):
        blk = yg[:, g * gw:(g + 1) * gw]
        normed.append(blk * lax.rsqrt(jnp.mean(blk * blk, axis=-1, keepdims=True) + RMS_EPS))
    y_ssm = jnp.concatenate(normed, axis=1).astype(BF16)
    p_ssm = jnp.dot(y_ssm, ws_ref[...], preferred_element_type=F32)
    gate_s = (jnp.tanh(gs_ref[...]) + 1.0).astype(F32)
    gate_a = (jnp.tanh(ga_ref[...]) + 1.0).astype(F32)
    merged = gate_s * p_ssm + gate_a * p_att
    out = jnp.dot(merged.astype(BF16), wo_ref[...], preferred_element_type=F32)
    ms = jnp.mean(out * out, axis=-1, keepdims=True)
    o_ref[...] = x_ref[...] + out * lax.rsqrt(ms + RMS_EPS) * pg_ref[...]


def _merge(y_scan, o_att, proj, x2, w_ssm, w_att, w_out, post_g, tm):
    m, d = x2.shape
    row = lambda i: (i, 0)
    full = lambda i: (0, 0)
    return pl.pallas_call(
        _merge_kernel,
        grid=(m // tm,),
        in_specs=[
            pl.BlockSpec((tm, d), row),
            pl.BlockSpec((tm, d), lambda i: (i, COL_Z_SSM)),
            pl.BlockSpec((tm, d), row),
            pl.BlockSpec((tm, d), lambda i: (i, COL_Z_ATT)),
            pl.BlockSpec((tm, d), lambda i: (i, COL_G_SSM)),
            pl.BlockSpec((tm, d), lambda i: (i, COL_G_ATT)),
            pl.BlockSpec((tm, d), row),
            pl.BlockSpec((d, d), full),
            pl.BlockSpec((d, d), full),
            pl.BlockSpec((d, d), full),
            pl.BlockSpec((1, d), full),
        ],
        out_specs=pl.BlockSpec((tm, d), row),
        out_shape=jax.ShapeDtypeStruct((m, d), F32),
        compiler_params=pltpu.CompilerParams(
            dimension_semantics=("arbitrary",), vmem_limit_bytes=VMEM_LIMIT),
        name="gated_merge",
    )(y_scan, proj, o_att, proj, proj, proj, x2, w_ssm, w_att, w_out, post_g)


def _head_lanes(v, slot):
    cols = [jnp.zeros_like(v, dtype=F32)] * 8
    cols[slot] = v.astype(F32)
    return jnp.stack(cols, axis=-1).reshape(1, LANES)


def _layer(x, pre_g, w_in, conv_w, conv_b, dt_bias, a_log, d_skip, ssm_norm_g, fgate_b,
           w_branch_ssm, w_branch_att, w_out, post_g):
    bsz, seqlen, d = x.shape
    n_heads = d // HEAD_DIM
    assert n_heads * 8 == LANES, "small-projection lane layout assumes 16 heads"
    gn = SSM_GROUPS * SSM_STATE
    sizes = (d, d + 2 * gn, n_heads, d, d, d, n_heads, d, d, d)
    offs = [0]
    for sz in sizes:
        offs.append(offs[-1] + sz)
    z_ssm, xbc, dt_w, q_w, k_w, v_w, f_w, z_att, g_ssm, g_att = (w_in[:, offs[n]:offs[n + 1]] for n in range(10))
    w_big = jnp.concatenate([0.5 * z_ssm, xbc, k_w, 0.5 * z_att, 0.5 * g_ssm, 0.5 * g_att], axis=1).astype(BF16)
    w_t = jnp.concatenate([(q_w * (HEAD_DIM ** -0.5 * LOG2E)).T, v_w.T], axis=0).astype(BF16)
    slots = [jnp.zeros_like(f_w)] * 8
    slots[F_LANE], slots[DT_LANE] = f_w, dt_w
    w_small = jnp.stack(slots, axis=-1).reshape(d, LANES).astype(BF16)

    x2 = x.reshape(bsz * seqlen, d)
    proj, small, q_t, vta, vtb = _in_proj(x2, pre_g.reshape(1, d), w_big, w_t, w_small, tm=min(1024, bsz * seqlen))

    gk, gq_t = _fcumsum(small, _head_lanes(fgate_b, F_LANE), bsz, seqlen)
    o_att = _attention(proj, q_t, vta, vtb, gk, gq_t, bsz, seqlen, d, tq=min(2048, seqlen))

    head_of_col = np.arange(d) // HEAD_DIM
    lane_ids = np.arange(LANES)[:, None]
    expand_m = jnp.asarray((lane_ids == 8 * head_of_col[None, :] + DT_LANE)
                           | (lane_ids == 8 * head_of_col[None, :] + DT_LANE + 1), BF16)
    t_ids = np.arange(SSM_CHUNK)[:, None]
    src = np.arange(2 * SSM_CHUNK)[None, :]
    shift_m = jnp.asarray(np.concatenate([src == np.where(t_ids >= k, t_ids - k, 2 * SSM_CHUNK + t_ids - k)
                                          for k in range(1, SSM_CONV)], axis=0), BF16)
    y_scan = _ssd(proj, small, conv_w, conv_b.reshape(1, -1), _head_lanes(dt_bias, DT_LANE), _head_lanes(a_log, DT_LANE),
                 jnp.repeat(d_skip.astype(F32), HEAD_DIM).reshape(1, d), expand_m, shift_m,
                 bsz, seqlen, d, n_heads)

    out = _merge(y_scan, o_att, proj, x2, (ssm_norm_g.astype(F32)[:, None] * w_branch_ssm).astype(BF16),
                 w_branch_att.astype(BF16), (0.5 * w_out).astype(BF16), post_g.reshape(1, d), tm=min(512, bsz * seqlen))
    return out.reshape(bsz, seqlen, d)


def kernel(x, pre_norm_g, w_in, conv_w, conv_b, dt_bias, a_log, d_skip, ssm_norm_g, fgate_b,
           w_branch_ssm, w_branch_att, w_out, post_norm_g):
    for i in range(pre_norm_g.shape[0]):
        x = _layer(x, pre_norm_g[i], w_in[i], conv_w[i], conv_b[i], dt_bias[i], a_log[i], d_skip[i],
                   ssm_norm_g[i], fgate_b[i], w_branch_ssm[i], w_branch_att[i], w_out[i], post_norm_g[i])
    return x
```

```python
import functools

import jax
import jax.numpy as jnp
import numpy as np
from jax import lax
from jax.experimental import pallas as pl
from jax.experimental.pallas import tpu as pltpu

F32 = jnp.float32
BF16 = jnp.bfloat16

RMS_EPS = 1e-6
HEAD_DIM = 64
SSM_GROUPS = 4
SSM_STATE = 128
SSM_CONV = 4
SSM_CHUNK = 128
CONV_COLS = 512
SSD_CHUNKS_PER_STEP = 4
NORM_ROWS = 256
KV_SUB = 512
LANES = 128
VMEM_LIMIT = 56 * 1024 * 1024

COL_Z_SSM, COL_XS, COL_BC, COL_K, COL_Z_ATT, COL_G_SSM, COL_G_ATT = range(7)
N_COL_BLOCKS = 7
PROJ_STEPS = 4
N_T_BLOCKS = 2
LOG2E = 1.4426950408889634
F_LANE = 0
DT_LANE = 4


def _sigmoid(x):
    return 0.5 * jnp.tanh(0.5 * x) + 0.5


def _silu(x):
    return x * _sigmoid(x)


def _softplus(x):
    return jnp.maximum(x, 0.0) + jnp.log(1.0 + jnp.exp(-jnp.abs(x)))


def _split3(x):
    hi = x.astype(BF16)
    r1 = x - hi.astype(F32)
    mid = r1.astype(BF16)
    lo = (r1 - mid.astype(F32)).astype(BF16)
    return hi, mid, lo


def _tri_cumsum(tri_bf16, x):
    hi, mid, lo = _split3(x)
    dot = functools.partial(jnp.dot, preferred_element_type=F32)
    return dot(tri_bf16, hi) + dot(tri_bf16, mid) + dot(tri_bf16, lo)


def _lower_tri(n):
    r = lax.broadcasted_iota(jnp.int32, (n, n), 0)
    c = lax.broadcasted_iota(jnp.int32, (n, n), 1)
    return r >= c


def _in_proj_kernel(x_ref, g_ref, w_ref, wt_ref, ws_ref, p_ref, s_ref, qt_ref, vta_ref, vtb_ref, h_ref):
    j = pl.program_id(1)
    nt = (((1,), (1,)), ((), ()))

    @pl.when(j == 0)
    def _():
        for r0 in range(0, x_ref.shape[0], NORM_ROWS):
            rows = slice(r0, r0 + NORM_ROWS)
            x = x_ref[rows, :]
            ms = jnp.mean(x * x, axis=-1, keepdims=True)
            h = (x * lax.rsqrt(ms + RMS_EPS) * g_ref[...]).astype(BF16)
            h_ref[rows, :] = h
            s_ref[rows, :] = jnp.dot(h, ws_ref[...], preferred_element_type=F32)
            p_ref[rows, :] = jnp.dot(h, w_ref[...], preferred_element_type=F32).astype(BF16)

    @pl.when((j > 0) & (j < PROJ_STEPS))
    def _():
        p_ref[...] = jnp.dot(h_ref[...], w_ref[...], preferred_element_type=F32).astype(BF16)

    @pl.when(j == PROJ_STEPS)
    def _():
        d = qt_ref.shape[0]
        qt_ref[...] = lax.dot_general(wt_ref[:d, :], h_ref[...], nt, preferred_element_type=F32).astype(BF16)
        vt = lax.dot_general(wt_ref[d:, :], h_ref[...], nt, preferred_element_type=F32)
        first = (lax.broadcasted_iota(jnp.int32, vt.shape, 0) % LANES) < HEAD_DIM
        vta = jnp.where(first, vt, 1.0).astype(BF16)
        vtb = jnp.where(first, 1.0, vt).astype(BF16)
        for c in range(vta_ref.shape[0]):
            vta_ref[c] = vta[:, c * KV_SUB:(c + 1) * KV_SUB]
            vtb_ref[c] = vtb[:, c * KV_SUB:(c + 1) * KV_SUB]


def _in_proj(x2, g, w_big, w_t, w_small, tm):
    m, d = x2.shape
    last = PROJ_STEPS - 1
    wide = N_COL_BLOCKS * d // PROJ_STEPS
    assert wide % LANES == 0 and wide * PROJ_STEPS == N_COL_BLOCKS * d
    sub = tm // KV_SUB
    vt_shape = jax.ShapeDtypeStruct((m // KV_SUB, d, KV_SUB), BF16)
    return pl.pallas_call(
        _in_proj_kernel,
        grid=(m // tm, PROJ_STEPS + 1),
        in_specs=[
            pl.BlockSpec((tm, d), lambda i, j: (i, 0)),
            pl.BlockSpec((1, d), lambda i, j: (0, 0)),
            pl.BlockSpec((d, wide), lambda i, j: (0, jnp.minimum(j, last))),
            pl.BlockSpec((N_T_BLOCKS * d, d), lambda i, j: (0, 0)),
            pl.BlockSpec((d, LANES), lambda i, j: (0, 0)),
        ],
        out_specs=[
            pl.BlockSpec((tm, wide), lambda i, j: (i, jnp.minimum(j, last))),
            pl.BlockSpec((tm, LANES), lambda i, j: (i, 0)),
            pl.BlockSpec((d, tm), lambda i, j: (0, i)),
            pl.BlockSpec((sub, d, KV_SUB), lambda i, j: (i, 0, 0)),
            pl.BlockSpec((sub, d, KV_SUB), lambda i, j: (i, 0, 0)),
        ],
        out_shape=[
            jax.ShapeDtypeStruct((m, N_COL_BLOCKS * d), BF16),
            jax.ShapeDtypeStruct((m, LANES), F32),
            jax.ShapeDtypeStruct((d, m), BF16),
            vt_shape,
            vt_shape,
        ],
        scratch_shapes=[pltpu.VMEM((tm, d), BF16)],
        compiler_params=pltpu.CompilerParams(
            dimension_semantics=("arbitrary", "arbitrary"), vmem_limit_bytes=VMEM_LIMIT),
        name="in_proj",
    )(x2, g, w_big, w_t, w_small)


def _fcumsum_kernel(s_ref, b_ref, gk_ref, gqt_ref):
    seqlen = s_ref.shape[0]
    blk = SSM_CHUNK
    tri = _lower_tri(blk).astype(BF16)
    slot = lax.broadcasted_iota(jnp.int32, (blk, LANES), 1) % 8
    carry = jnp.zeros((1, LANES), F32)
    for c in range(seqlen // blk):
        rows = slice(c * blk, (c + 1) * blk)
        raw = s_ref[rows, :] + b_ref[...]
        logf = jnp.where(slot == F_LANE, -_softplus(-raw), 0.0)
        f_blk = _tri_cumsum(tri, logf) + carry
        carry = f_blk[blk - 1:blk, :]
        hi, mid, lo = (p.astype(F32) for p in _split3(f_blk * LOG2E))
        gq = hi + pltpu.roll(mid, 1, axis=1) + pltpu.roll(lo, 2, axis=1) + jnp.where((slot >= 3) & (slot < 6), 1.0, 0.0)
        gk = jnp.where(slot < 3, 1.0, 0.0) - (pltpu.roll(hi, 3, axis=1) + pltpu.roll(mid, 4, axis=1)
                                              + pltpu.roll(lo, 5, axis=1))
        gk_ref[rows, :] = gk.astype(BF16)
        gqt_ref[:, rows] = gq.T.astype(BF16)


def _fcumsum(small, fb_vec, bsz, seqlen):
    return pl.pallas_call(
        _fcumsum_kernel,
        grid=(bsz,),
        in_specs=[
            pl.BlockSpec((seqlen, LANES), lambda b: (b, 0)),
            pl.BlockSpec((1, LANES), lambda b: (0, 0)),
        ],
        out_specs=[
            pl.BlockSpec((seqlen, LANES), lambda b: (b, 0)),
            pl.BlockSpec((LANES, seqlen), lambda b: (b, 0)),
        ],
        out_shape=[
            jax.ShapeDtypeStruct((bsz * seqlen, LANES), BF16),
            jax.ShapeDtypeStruct((bsz * LANES, seqlen), BF16),
        ],
        compiler_params=pltpu.CompilerParams(
            dimension_semantics=("arbitrary",), vmem_limit_bytes=VMEM_LIMIT),
        name="fcumsum",
    )(small, fb_vec)


def _attn_kernel(qt_ref, k_ref, gk_ref, vta_ref, vtb_ref, gqt_ref, o_ref,
                 qaug_ref, s_a0, s_a1, s_b0, s_b1, acc_ref, m_ref, *, tq):
    hp = pl.program_id(1)
    i = pl.program_id(2)
    tk = KV_SUB
    sub_per_q = tq // tk
    s_refs = ((s_a0, s_a1), (s_b0, s_b1))
    vt_refs = (vta_ref, vtb_ref)

    grp = lax.broadcasted_iota(jnp.int32, (LANES, tq), 0) // 8
    gq_t = gqt_ref[...].astype(F32)
    zeros_half = jnp.zeros((HEAD_DIM, tq), BF16)
    qaug_ref[0, :HEAD_DIM] = qt_ref[:HEAD_DIM, :]
    qaug_ref[0, HEAD_DIM:LANES] = zeros_half
    qaug_ref[1, :HEAD_DIM] = zeros_half
    qaug_ref[1, HEAD_DIM:LANES] = qt_ref[HEAD_DIM:, :]
    for h in range(2):
        qaug_ref[h, LANES:] = jnp.where(grp == 2 * hp + h, gq_t, 0.0).astype(BF16)
    acc_ref[...] = jnp.zeros(acc_ref.shape, F32)
    m_ref[...] = jnp.full(m_ref.shape, -jnp.inf, F32)

    def qk(sub, n0, slot):
        rows = pl.ds(pl.multiple_of(sub * tk, tk), tk)
        k_sub = jnp.concatenate([k_ref[rows, :], gk_ref[rows, :]], axis=1)
        for h in range(2):
            s_refs[h][slot][:, n0:] = jnp.dot(k_sub, qaug_ref[h, :, n0:], preferred_element_type=F32)

    def update(sub, n0, slot, keep=None):
        for h in range(2):
            s = s_refs[h][slot][:, n0:]
            if keep is not None:
                diag = jnp.where(keep, s[:, :tk], -jnp.inf)
                s = diag if n0 + tk == tq else jnp.concatenate([diag, s[:, tk:]], axis=1)
            m = m_ref[h, 0:1, n0:]
            mn = jnp.maximum(m, jnp.max(s, axis=0, keepdims=True))
            p = jnp.exp2(s - mn).astype(BF16)
            acc_ref[h, :, n0:] = (acc_ref[h, :, n0:] * jnp.exp2(m - mn)
                                  + jnp.dot(vt_refs[h][sub], p, preferred_element_type=F32))
            m_ref[h, 0:1, n0:] = mn

    first_diag = i * sub_per_q
    qk(0, 0, 0)

    def body(j, carry):
        for c in range(sub_per_q):
            sub = j * sub_per_q + c
            qk(sub + 1, 0, (c + 1) % 2)
            update(sub, 0, c % 2)
        return carry

    lax.fori_loop(0, i, body, 0)

    keep = lax.broadcasted_iota(jnp.int32, (tk, tk), 0) <= lax.broadcasted_iota(jnp.int32, (tk, tk), 1)
    for c in range(sub_per_q):
        n0 = c * tk
        if c + 1 < sub_per_q:
            qk(first_diag + c + 1, n0 + tk, (c + 1) % 2)
        update(first_diag + c, n0, c % 2, keep)

    inv_a = 1.0 / acc_ref[0, HEAD_DIM:HEAD_DIM + 1, :]
    inv_b = 1.0 / acc_ref[1, 0:1, :]
    o_t = jnp.concatenate([acc_ref[0, :HEAD_DIM, :] * inv_a, acc_ref[1, HEAD_DIM:, :] * inv_b], axis=0)
    o_ref[...] = o_t.T.astype(BF16)


def _attention(proj, q_t, vta, vtb, gk, gq_t, bsz, seqlen, d_model, tq):
    assert (tq // KV_SUB) % 2 == 0, "score buffers alternate per key sub-tile"
    n_hp = d_model // LANES
    per_row = d_model // LANES
    nq = seqlen // tq
    n_sub = seqlen // KV_SUB
    score_buf = pltpu.VMEM((KV_SUB, tq), F32)
    return pl.pallas_call(
        functools.partial(_attn_kernel, tq=tq),
        grid=(bsz, n_hp, nq),
        in_specs=[
            pl.BlockSpec((LANES, tq), lambda b, h, i: (h, b * nq + i)),
            pl.BlockSpec((seqlen, LANES), lambda b, h, i: (b, COL_K * per_row + h)),
            pl.BlockSpec((seqlen, LANES), lambda b, h, i: (b, 0)),
            pl.BlockSpec((n_sub, LANES, KV_SUB), lambda b, h, i: (b, h, 0)),
            pl.BlockSpec((n_sub, LANES, KV_SUB), lambda b, h, i: (b, h, 0)),
            pl.BlockSpec((LANES, tq), lambda b, h, i: (b, i)),
        ],
        out_specs=pl.BlockSpec((tq, LANES), lambda b, h, i: (b * nq + i, h)),
        out_shape=jax.ShapeDtypeStruct((bsz * seqlen, d_model), BF16),
        scratch_shapes=[
            pltpu.VMEM((2, 2 * LANES, tq), BF16),
            score_buf, score_buf, score_buf, score_buf,
            pltpu.VMEM((2, LANES, tq), F32),
            pltpu.VMEM((2, 8, tq), F32),
        ],
        compiler_params=pltpu.CompilerParams(
            dimension_semantics=("arbitrary", "arbitrary", "arbitrary"), vmem_limit_bytes=VMEM_LIMIT),
        name="fox_attention",
    )(q_t, proj, gk, vta, vtb, gq_t)


def _ssd_chunk(is_first, xs_ref, bc_ref, s_ref, cw_ref, cb_ref, dtb_ref, alog_ref, dskip_ref,
               expand_ref, shift_ref, y_ref, prev_ref, st_ref, n_heads):
    q = SSM_CHUNK
    d_inner = xs_ref.shape[1]
    gn = SSM_GROUPS * SSM_STATE
    heads_per_group = n_heads // SSM_GROUPS
    gw = heads_per_group * HEAD_DIM

    if is_first is not None:
        @pl.when(is_first)
        def _():
            prev_ref[...] = jnp.zeros(prev_ref.shape, BF16)
            st_ref[...] = jnp.zeros(st_ref.shape, F32)

    cur = jnp.concatenate([xs_ref[...], bc_ref[...]], axis=1)
    both = jnp.concatenate([cur, prev_ref[...]], axis=0)
    prev_ref[...] = cur
    acts = []
    for n in range(cur.shape[1] // CONV_COLS):
        cols = slice(n * CONV_COLS, (n + 1) * CONV_COLS)
        shifted = jnp.dot(shift_ref[...], both[:, cols], preferred_element_type=F32)
        conv = cb_ref[:, cols] + cw_ref[SSM_CONV - 1:SSM_CONV, cols] * cur[:, cols].astype(F32)
        for jj in range(SSM_CONV - 1):
            k = SSM_CONV - 1 - jj
            conv = conv + cw_ref[jj:jj + 1, cols] * shifted[(k - 1) * q:k * q, :]
        acts.append(_silu(conv))
        yield
    act = jnp.concatenate(acts, axis=1)
    xs = act[:, :d_inner]
    b16 = act[:, d_inner:d_inner + gn].astype(BF16)
    c16 = act[:, d_inner + gn:].astype(BF16)

    lane = lax.broadcasted_iota(jnp.int32, (q, LANES), 1)
    head_lane = lane % 8 == DT_LANE
    dt = jnp.where(head_lane, _softplus(s_ref[...] + dtb_ref[...]), 0.0)
    da = dt * (-jnp.exp(alog_ref[...]))
    tri = _lower_tri(q)
    cs = _tri_cumsum(tri.astype(BF16), da)
    yield
    cs2 = cs * LOG2E
    key_t = (cs2 - jnp.log2(dt)).T
    last = cs[q - 1:q, :]
    w_end = dt * jnp.exp(last - cs)
    ecs = jnp.exp(cs)
    yield

    def expand(v):
        hi, mid, _ = _split3(jnp.where(head_lane, v, 0.0))
        both = hi.astype(F32) + pltpu.roll(mid.astype(F32), 1, axis=1)
        return jnp.dot(both.astype(BF16), expand_ref[...], preferred_element_type=F32)

    w_e = expand(w_end)
    ecs_e = expand(ecs)
    xw16 = (xs * w_e).astype(BF16)
    yield

    xs16 = xs.astype(BF16)
    first_half = lax.broadcasted_iota(jnp.int32, (q, LANES), 1) < HEAD_DIM

    y_parts = []
    for g in range(SSM_GROUPS):
        bg = b16[:, g * SSM_STATE:(g + 1) * SSM_STATE]
        cg = c16[:, g * SSM_STATE:(g + 1) * SSM_STATE]
        cb = lax.dot_general(cg, bg, (((1,), (1,)), ((), ())), preferred_element_type=F32)
        st_prev = st_ref[g]
        y_off = jnp.dot(cg, st_prev.astype(BF16), preferred_element_type=F32)
        yield
        for pair in range(heads_per_group // 2):
            lo_lane = g * gw + pair * LANES
            halves = []
            for half in range(2):
                h = 8 * (g * heads_per_group + pair * 2 + half) + DT_LANE
                seg = cs2[:, h:h + 1] - key_t[h:h + 1, :]
                mmat = (cb * jnp.exp2(jnp.where(tri, seg, -jnp.inf))).astype(BF16)
                halves.append(jnp.dot(mmat, xs16[:, lo_lane:lo_lane + LANES], preferred_element_type=F32))
            y_pair = jnp.where(first_half, halves[0], halves[1])
            y_pair = y_pair + y_off[:, pair * LANES:(pair + 1) * LANES] * ecs_e[:, lo_lane:lo_lane + LANES]
            y_parts.append(y_pair)
            yield
        bg_t = bg.astype(F32).T.astype(BF16)
        new_st = jnp.dot(bg_t, xw16[:, g * gw:(g + 1) * gw], preferred_element_type=F32)
        st_ref[g] = st_prev * ecs_e[q - 1:q, g * gw:(g + 1) * gw] + new_st
        yield

    y_ref[...] = (jnp.concatenate(y_parts, axis=1) + xs * dskip_ref[...]).astype(BF16)
    yield


def _ssd_kernel(xs_ref, bc_ref, s_ref, cw_ref, cb_ref, dtb_ref, alog_ref, dskip_ref,
                expand_ref, shift_ref, y_ref, prev_ref, st_ref, *, n_heads):
    c = pl.program_id(1)
    q = SSM_CHUNK
    for cc in range(xs_ref.shape[1] // q):
        rows = slice(cc * q, (cc + 1) * q)
        is_first = (c == 0) if cc == 0 else None
        chunks = [_ssd_chunk(is_first, xs_ref.at[seq, rows], bc_ref.at[seq, rows], s_ref.at[seq, rows], cw_ref, cb_ref,
                             dtb_ref, alog_ref, dskip_ref, expand_ref, shift_ref, y_ref.at[seq, rows],
                             prev_ref.at[seq], st_ref.at[seq], n_heads)
                  for seq in range(xs_ref.shape[0])]
        for _ in zip(*chunks):
            pass


def _ssd(proj, small, conv_w, conv_b, dtb_vec, alog_vec, dskip_e, expand_m, shift_m, bsz, seqlen, d_model, n_heads):
    q = SSM_CHUNK
    conv_dim = conv_w.shape[1]
    gw = (n_heads // SSM_GROUPS) * HEAD_DIM
    nb = 2 if bsz % 2 == 0 else 1
    rows = SSD_CHUNKS_PER_STEP * q
    assert seqlen % rows == 0
    proj3 = proj.reshape(bsz, seqlen, proj.shape[1])
    const = lambda b, c: (0, 0)
    y = pl.pallas_call(
        functools.partial(_ssd_kernel, n_heads=n_heads),
        grid=(bsz // nb, seqlen // rows),
        in_specs=[
            pl.BlockSpec((nb, rows, d_model), lambda b, c: (b, c, COL_XS)),
            pl.BlockSpec((nb, rows, d_model), lambda b, c: (b, c, COL_BC)),
            pl.BlockSpec((nb, rows, LANES), lambda b, c: (b, c, 0)),
            pl.BlockSpec((SSM_CONV, conv_dim), const),
            pl.BlockSpec((1, conv_dim), const),
            pl.BlockSpec((1, LANES), const),
            pl.BlockSpec((1, LANES), const),
            pl.BlockSpec((1, d_model), const),
            pl.BlockSpec((LANES, d_model), const),
            pl.BlockSpec(((SSM_CONV - 1) * q, 2 * q), const),
        ],
        out_specs=pl.BlockSpec((nb, rows, d_model), lambda b, c: (b, c, 0)),
        out_shape=jax.ShapeDtypeStruct((bsz, seqlen, d_model), BF16),
        scratch_shapes=[
            pltpu.VMEM((nb, q, conv_dim), BF16),
            pltpu.VMEM((nb, SSM_GROUPS, SSM_STATE, gw), F32),
        ],
        compiler_params=pltpu.CompilerParams(
            dimension_semantics=("arbitrary", "arbitrary"), vmem_limit_bytes=VMEM_LIMIT),
        name="ssd_branch",
    )(proj3, proj3, small.reshape(bsz, seqlen, LANES), conv_w, conv_b, dtb_vec, alog_vec, dskip_e, expand_m, shift_m)
    return y.reshape(bsz * seqlen, d_model)


def _half_silu_gate(v16, zh16):
    return v16 * zh16 * (jnp.tanh(zh16) + 1.0)


def _merge_kernel(ys_ref, zs_ref, oa_ref, za_ref, gs_ref, ga_ref, x_ref, ws_ref, wa_ref, wo_ref, pg_ref, o_ref):
    y_att = _half_silu_gate(oa_ref[...], za_ref[...])
    p_att = jnp.dot(y_att, wa_ref[...], preferred_element_type=F32)
    yg = _half_silu_gate(ys_ref[...], zs_ref[...]).astype(F32)
    gw = yg.shape[1] // SSM_GROUPS
    normed = []
    for g in range(SSM_GROUPS):
        blk = yg[:, g * gw:(g + 1) * gw]
        normed.append(blk * lax.rsqrt(jnp.mean(blk * blk, axis=-1, keepdims=True) + RMS_EPS))
    y_ssm = jnp.concatenate(normed, axis=1).astype(BF16)
    p_ssm = jnp.dot(y_ssm, ws_ref[...], preferred_element_type=F32)
    gate_s = (jnp.tanh(gs_ref[...]) + 1.0).astype(F32)
    gate_a = (jnp.tanh(ga_ref[...]) + 1.0).astype(F32)
    merged = gate_s * p_ssm + gate_a * p_att
    out = jnp.dot(merged.astype(BF16), wo_ref[...], preferred_element_type=F32)
    ms = jnp.mean(out * out, axis=-1, keepdims=True)
    o_ref[...] = x_ref[...] + out * lax.rsqrt(ms + RMS_EPS) * pg_ref[...]


def _merge(y_scan, o_att, proj, x2, w_ssm, w_att, w_out, post_g, tm):
    m, d = x2.shape
    row = lambda i: (i, 0)
    full = lambda i: (0, 0)
    return pl.pallas_call(
        _merge_kernel,
        grid=(m // tm,),
        in_specs=[
            pl.BlockSpec((tm, d), row),
            pl.BlockSpec((tm, d), lambda i: (i, COL_Z_SSM)),
            pl.BlockSpec((tm, d), row),
            pl.BlockSpec((tm, d), lambda i: (i, COL_Z_ATT)),
            pl.BlockSpec((tm, d), lambda i: (i, COL_G_SSM)),
            pl.BlockSpec((tm, d), lambda i: (i, COL_G_ATT)),
            pl.BlockSpec((tm, d), row),
            pl.BlockSpec((d, d), full),
            pl.BlockSpec((d, d), full),
            pl.BlockSpec((d, d), full),
            pl.BlockSpec((1, d), full),
        ],
        out_specs=pl.BlockSpec((tm, d), row),
        out_shape=jax.ShapeDtypeStruct((m, d), F32),
        compiler_params=pltpu.CompilerParams(
            dimension_semantics=("arbitrary",), vmem_limit_bytes=VMEM_LIMIT),
        name="gated_merge",
    )(y_scan, proj, o_att, proj, proj, proj, x2, w_ssm, w_att, w_out, post_g)


def _head_lanes(v, slot):
    cols = [jnp.zeros_like(v, dtype=F32)] * 8
    cols[slot] = v.astype(F32)
    return jnp.stack(cols, axis=-1).reshape(1, LANES)


def _layer(x, pre_g, w_in, conv_w, conv_b, dt_bias, a_log, d_skip, ssm_norm_g, fgate_b,
           w_branch_ssm, w_branch_att, w_out, post_g):
    bsz, seqlen, d = x.shape
    n_heads = d // HEAD_DIM
    assert n_heads * 8 == LANES, "small-projection lane layout assumes 16 heads"
    gn = SSM_GROUPS * SSM_STATE
    sizes = (d, d + 2 * gn, n_heads, d, d, d, n_heads, d, d, d)
    offs = [0]
    for sz in sizes:
        offs.append(offs[-1] + sz)
    z_ssm, xbc, dt_w, q_w, k_w, v_w, f_w, z_att, g_ssm, g_att = (w_in[:, offs[n]:offs[n + 1]] for n in range(10))
    w_big = jnp.concatenate([0.5 * z_ssm, xbc, k_w, 0.5 * z_att, 0.5 * g_ssm, 0.5 * g_att], axis=1).astype(BF16)
    w_t = jnp.concatenate([(q_w * (HEAD_DIM ** -0.5 * LOG2E)).T, v_w.T], axis=0).astype(BF16)
    slots = [jnp.zeros_like(f_w)] * 8
    slots[F_LANE], slots[DT_LANE] = f_w, dt_w
    w_small = jnp.stack(slots, axis=-1).reshape(d, LANES).astype(BF16)

    x2 = x.reshape(bsz * seqlen, d)
    proj, small, q_t, vta, vtb = _in_proj(x2, pre_g.reshape(1, d), w_big, w_t, w_small, tm=min(1024, bsz * seqlen))

    gk, gq_t = _fcumsum(small, _head_lanes(fgate_b, F_LANE), bsz, seqlen)
    o_att = _attention(proj, q_t, vta, vtb, gk, gq_t, bsz, seqlen, d, tq=min(2048, seqlen))

    head_of_col = np.arange(d) // HEAD_DIM
    lane_ids = np.arange(LANES)[:, None]
    expand_m = jnp.asarray((lane_ids == 8 * head_of_col[None, :] + DT_LANE)
                           | (lane_ids == 8 * head_of_col[None, :] + DT_LANE + 1), BF16)
    t_ids = np.arange(SSM_CHUNK)[:, None]
    src = np.arange(2 * SSM_CHUNK)[None, :]
    shift_m = jnp.asarray(np.concatenate([src == np.where(t_ids >= k, t_ids - k, 2 * SSM_CHUNK + t_ids - k)
                                          for k in range(1, SSM_CONV)], axis=0), BF16)
    y_scan = _ssd(proj, small, conv_w, conv_b.reshape(1, -1), _head_lanes(dt_bias, DT_LANE), _head_lanes(a_log, DT_LANE),
                 jnp.repeat(d_skip.astype(F32), HEAD_DIM).reshape(1, d), expand_m, shift_m,
                 bsz, seqlen, d, n_heads)

    out = _merge(y_scan, o_att, proj, x2, (ssm_norm_g.astype(F32)[:, None] * w_branch_ssm).astype(BF16),
                 w_branch_att.astype(BF16), (0.5 * w_out).astype(BF16), post_g.reshape(1, d), tm=min(512, bsz * seqlen))
    return out.reshape(bsz, seqlen, d)


def kernel(x, pre_norm_g, w_in, conv_w, conv_b, dt_bias, a_log, d_skip, ssm_norm_g, fgate_b,
           w_branch_ssm, w_branch_att, w_out, post_norm_g):
    for i in range(pre_norm_g.shape[0]):
        x = _layer(x, pre_norm_g[i], w_in[i], conv_w[i], conv_b[i], dt_bias[i], a_log[i], d_skip[i],
                   ssm_norm_g[i], fgate_b[i], w_branch_ssm[i], w_branch_att[i], w_out[i], post_norm_g[i])
    return x
```

```python
import functools

import jax
import jax.numpy as jnp
import numpy as np
from jax import lax
from jax.experimental import pallas as pl
from jax.experimental.pallas import tpu as pltpu

F32 = jnp.float32
BF16 = jnp.bfloat16

RMS_EPS = 1e-6
HEAD_DIM = 64
SSM_GROUPS = 4
SSM_STATE = 128
SSM_CONV = 4
SSM_CHUNK = 128
CONV_COLS = 512
SSD_CHUNKS_PER_STEP = 8
NORM_ROWS = 256
KV_SUB = 512
LANES = 128
VMEM_LIMIT = 56 * 1024 * 1024

COL_Z_SSM, COL_XS, COL_BC, COL_K, COL_Z_ATT, COL_G_SSM, COL_G_ATT = range(7)
N_COL_BLOCKS = 7
PROJ_STEPS = 4
N_T_BLOCKS = 2
LOG2E = 1.4426950408889634
F_LANE = 0
DT_LANE = 4


def _sigmoid(x):
    return 0.5 * jnp.tanh(0.5 * x) + 0.5


def _silu(x):
    return x * _sigmoid(x)


def _softplus(x):
    return jnp.maximum(x, 0.0) + jnp.log(1.0 + jnp.exp(-jnp.abs(x)))


def _split3(x):
    hi = x.astype(BF16)
    r1 = x - hi.astype(F32)
    mid = r1.astype(BF16)
    lo = (r1 - mid.astype(F32)).astype(BF16)
    return hi, mid, lo


def _tri_cumsum(tri_bf16, x):
    hi, mid, lo = _split3(x)
    dot = functools.partial(jnp.dot, preferred_element_type=F32)
    return dot(tri_bf16, hi) + dot(tri_bf16, mid) + dot(tri_bf16, lo)


def _lower_tri(n):
    r = lax.broadcasted_iota(jnp.int32, (n, n), 0)
    c = lax.broadcasted_iota(jnp.int32, (n, n), 1)
    return r >= c


def _in_proj_kernel(x_ref, g_ref, w_ref, wt_ref, ws_ref, p_ref, s_ref, qt_ref, vta_ref, vtb_ref, h_ref):
    j = pl.program_id(1)
    nt = (((1,), (1,)), ((), ()))

    @pl.when(j == 0)
    def _():
        for r0 in range(0, x_ref.shape[0], NORM_ROWS):
            rows = slice(r0, r0 + NORM_ROWS)
            x = x_ref[rows, :]
            ms = jnp.mean(x * x, axis=-1, keepdims=True)
            h = (x * lax.rsqrt(ms + RMS_EPS) * g_ref[...]).astype(BF16)
            h_ref[rows, :] = h
            s_ref[rows, :] = jnp.dot(h, ws_ref[...], preferred_element_type=F32)
            p_ref[rows, :] = jnp.dot(h, w_ref[...], preferred_element_type=F32).astype(BF16)

    @pl.when((j > 0) & (j < PROJ_STEPS))
    def _():
        p_ref[...] = jnp.dot(h_ref[...], w_ref[...], preferred_element_type=F32).astype(BF16)

    @pl.when(j == PROJ_STEPS)
    def _():
        d = qt_ref.shape[0]
        qt_ref[...] = lax.dot_general(wt_ref[:d, :], h_ref[...], nt, preferred_element_type=F32).astype(BF16)
        vt = lax.dot_general(wt_ref[d:, :], h_ref[...], nt, preferred_element_type=F32)
        first = (lax.broadcasted_iota(jnp.int32, vt.shape, 0) % LANES) < HEAD_DIM
        vta = jnp.where(first, vt, 1.0).astype(BF16)
        vtb = jnp.where(first, 1.0, vt).astype(BF16)
        for c in range(vta_ref.shape[0]):
            vta_ref[c] = vta[:, c * KV_SUB:(c + 1) * KV_SUB]
            vtb_ref[c] = vtb[:, c * KV_SUB:(c + 1) * KV_SUB]


def _in_proj(x2, g, w_big, w_t, w_small, tm):
    m, d = x2.shape
    last = PROJ_STEPS - 1
    wide = N_COL_BLOCKS * d // PROJ_STEPS
    assert wide % LANES == 0 and wide * PROJ_STEPS == N_COL_BLOCKS * d
    sub = tm // KV_SUB
    vt_shape = jax.ShapeDtypeStruct((m // KV_SUB, d, KV_SUB), BF16)
    return pl.pallas_call(
        _in_proj_kernel,
        grid=(m // tm, PROJ_STEPS + 1),
        in_specs=[
            pl.BlockSpec((tm, d), lambda i, j: (i, 0)),
            pl.BlockSpec((1, d), lambda i, j: (0, 0)),
            pl.BlockSpec((d, wide), lambda i, j: (0, jnp.minimum(j, last))),
            pl.BlockSpec((N_T_BLOCKS * d, d), lambda i, j: (0, 0)),
            pl.BlockSpec((d, LANES), lambda i, j: (0, 0)),
        ],
        out_specs=[
            pl.BlockSpec((tm, wide), lambda i, j: (i, jnp.minimum(j, last))),
            pl.BlockSpec((tm, LANES), lambda i, j: (i, 0)),
            pl.BlockSpec((d, tm), lambda i, j: (0, i)),
            pl.BlockSpec((sub, d, KV_SUB), lambda i, j: (i, 0, 0)),
            pl.BlockSpec((sub, d, KV_SUB), lambda i, j: (i, 0, 0)),
        ],
        out_shape=[
            jax.ShapeDtypeStruct((m, N_COL_BLOCKS * d), BF16),
            jax.ShapeDtypeStruct((m, LANES), F32),
            jax.ShapeDtypeStruct((d, m), BF16),
            vt_shape,
            vt_shape,
        ],
        scratch_shapes=[pltpu.VMEM((tm, d), BF16)],
        compiler_params=pltpu.CompilerParams(
            dimension_semantics=("arbitrary", "arbitrary"), vmem_limit_bytes=VMEM_LIMIT),
        name="in_proj",
    )(x2, g, w_big, w_t, w_small)


def _fcumsum_kernel(s_ref, b_ref, gk_ref, gqt_ref):
    seqlen = s_ref.shape[0]
    blk = SSM_CHUNK
    tri = _lower_tri(blk).astype(BF16)
    slot = lax.broadcasted_iota(jnp.int32, (blk, LANES), 1) % 8
    carry = jnp.zeros((1, LANES), F32)
    for c in range(seqlen // blk):
        rows = slice(c * blk, (c + 1) * blk)
        raw = s_ref[rows, :] + b_ref[...]
        logf = jnp.where(slot == F_LANE, -_softplus(-raw), 0.0)
        f_blk = _tri_cumsum(tri, logf) + carry
        carry = f_blk[blk - 1:blk, :]
        hi, mid, lo = (p.astype(F32) for p in _split3(f_blk * LOG2E))
        gq = hi + pltpu.roll(mid, 1, axis=1) + pltpu.roll(lo, 2, axis=1) + jnp.where((slot >= 3) & (slot < 6), 1.0, 0.0)
        gk = jnp.where(slot < 3, 1.0, 0.0) - (pltpu.roll(hi, 3, axis=1) + pltpu.roll(mid, 4, axis=1)
                                              + pltpu.roll(lo, 5, axis=1))
        gk_ref[rows, :] = gk.astype(BF16)
        gqt_ref[:, rows] = gq.T.astype(BF16)


def _fcumsum(small, fb_vec, bsz, seqlen):
    return pl.pallas_call(
        _fcumsum_kernel,
        grid=(bsz,),
        in_specs=[
            pl.BlockSpec((seqlen, LANES), lambda b: (b, 0)),
            pl.BlockSpec((1, LANES), lambda b: (0, 0)),
        ],
        out_specs=[
            pl.BlockSpec((seqlen, LANES), lambda b: (b, 0)),
            pl.BlockSpec((LANES, seqlen), lambda b: (b, 0)),
        ],
        out_shape=[
            jax.ShapeDtypeStruct((bsz * seqlen, LANES), BF16),
            jax.ShapeDtypeStruct((bsz * LANES, seqlen), BF16),
        ],
        compiler_params=pltpu.CompilerParams(
            dimension_semantics=("arbitrary",), vmem_limit_bytes=VMEM_LIMIT),
        name="fcumsum",
    )(small, fb_vec)


def _attn_kernel(qt_ref, k_ref, gk_ref, vta_ref, vtb_ref, gqt_ref, o_ref,
                 qaug_ref, s_a0, s_a1, s_b0, s_b1, acc_ref, m_ref, *, tq):
    hp = pl.program_id(1)
    i = pl.program_id(2)
    tk = KV_SUB
    sub_per_q = tq // tk
    s_refs = ((s_a0, s_a1), (s_b0, s_b1))
    vt_refs = (vta_ref, vtb_ref)

    grp = lax.broadcasted_iota(jnp.int32, (LANES, tq), 0) // 8
    gq_t = gqt_ref[...].astype(F32)
    zeros_half = jnp.zeros((HEAD_DIM, tq), BF16)
    qaug_ref[0, :HEAD_DIM] = qt_ref[:HEAD_DIM, :]
    qaug_ref[0, HEAD_DIM:LANES] = zeros_half
    qaug_ref[1, :HEAD_DIM] = zeros_half
    qaug_ref[1, HEAD_DIM:LANES] = qt_ref[HEAD_DIM:, :]
    for h in range(2):
        qaug_ref[h, LANES:] = jnp.where(grp == 2 * hp + h, gq_t, 0.0).astype(BF16)
    acc_ref[...] = jnp.zeros(acc_ref.shape, F32)
    m_ref[...] = jnp.full(m_ref.shape, -jnp.inf, F32)

    def qk(sub, n0, slot):
        rows = pl.ds(pl.multiple_of(sub * tk, tk), tk)
        k_sub = jnp.concatenate([k_ref[rows, :], gk_ref[rows, :]], axis=1)
        for h in range(2):
            s_refs[h][slot][:, n0:] = jnp.dot(k_sub, qaug_ref[h, :, n0:], preferred_element_type=F32)

    def update(sub, n0, slot, keep=None):
        for h in range(2):
            s = s_refs[h][slot][:, n0:]
            if keep is not None:
                diag = jnp.where(keep, s[:, :tk], -jnp.inf)
                s = diag if n0 + tk == tq else jnp.concatenate([diag, s[:, tk:]], axis=1)
            m = m_ref[h, 0:1, n0:]
            mn = jnp.maximum(m, jnp.max(s, axis=0, keepdims=True))
            p = jnp.exp2(s - mn).astype(BF16)
            acc_ref[h, :, n0:] = (acc_ref[h, :, n0:] * jnp.exp2(m - mn)
                                  + jnp.dot(vt_refs[h][sub], p, preferred_element_type=F32))
            m_ref[h, 0:1, n0:] = mn

    first_diag = i * sub_per_q
    qk(0, 0, 0)

    def body(j, carry):
        for c in range(sub_per_q):
            sub = j * sub_per_q + c
            qk(sub + 1, 0, (c + 1) % 2)
            update(sub, 0, c % 2)
        return carry

    lax.fori_loop(0, i, body, 0)

    keep = lax.broadcasted_iota(jnp.int32, (tk, tk), 0) <= lax.broadcasted_iota(jnp.int32, (tk, tk), 1)
    for c in range(sub_per_q):
        n0 = c * tk
        if c + 1 < sub_per_q:
            qk(first_diag + c + 1, n0 + tk, (c + 1) % 2)
        update(first_diag + c, n0, c % 2, keep)

    inv_a = 1.0 / acc_ref[0, HEAD_DIM:HEAD_DIM + 1, :]
    inv_b = 1.0 / acc_ref[1, 0:1, :]
    o_t = jnp.concatenate([acc_ref[0, :HEAD_DIM, :] * inv_a, acc_ref[1, HEAD_DIM:, :] * inv_b], axis=0)
    o_ref[...] = o_t.T.astype(BF16)


def _attention(proj, q_t, vta, vtb, gk, gq_t, bsz, seqlen, d_model, tq):
    assert (tq // KV_SUB) % 2 == 0, "score buffers alternate per key sub-tile"
    n_hp = d_model // LANES
    per_row = d_model // LANES
    nq = seqlen // tq
    n_sub = seqlen // KV_SUB
    score_buf = pltpu.VMEM((KV_SUB, tq), F32)
    return pl.pallas_call(
        functools.partial(_attn_kernel, tq=tq),
        grid=(bsz, n_hp, nq),
        in_specs=[
            pl.BlockSpec((LANES, tq), lambda b, h, i: (h, b * nq + i)),
            pl.BlockSpec((seqlen, LANES), lambda b, h, i: (b, COL_K * per_row + h)),
            pl.BlockSpec((seqlen, LANES), lambda b, h, i: (b, 0)),
            pl.BlockSpec((n_sub, LANES, KV_SUB), lambda b, h, i: (b, h, 0)),
            pl.BlockSpec((n_sub, LANES, KV_SUB), lambda b, h, i: (b, h, 0)),
            pl.BlockSpec((LANES, tq), lambda b, h, i: (b, i)),
        ],
        out_specs=pl.BlockSpec((tq, LANES), lambda b, h, i: (b * nq + i, h)),
        out_shape=jax.ShapeDtypeStruct((bsz * seqlen, d_model), BF16),
        scratch_shapes=[
            pltpu.VMEM((2, 2 * LANES, tq), BF16),
            score_buf, score_buf, score_buf, score_buf,
            pltpu.VMEM((2, LANES, tq), F32),
            pltpu.VMEM((2, 8, tq), F32),
        ],
        compiler_params=pltpu.CompilerParams(
            dimension_semantics=("arbitrary", "arbitrary", "arbitrary"), vmem_limit_bytes=VMEM_LIMIT),
        name="fox_attention",
    )(q_t, proj, gk, vta, vtb, gq_t)


def _ssd_chunk(is_first, xs_ref, bc_ref, s_ref, cw_ref, cb_ref, dtb_ref, alog_ref, dskip_ref,
               expand_ref, shift_ref, y_ref, prev_ref, st_ref, n_heads):
    q = SSM_CHUNK
    d_inner = xs_ref.shape[1]
    gn = SSM_GROUPS * SSM_STATE
    heads_per_group = n_heads // SSM_GROUPS
    gw = heads_per_group * HEAD_DIM

    if is_first is not None:
        @pl.when(is_first)
        def _():
            prev_ref[...] = jnp.zeros(prev_ref.shape, BF16)
            st_ref[...] = jnp.zeros(st_ref.shape, F32)

    cur = jnp.concatenate([xs_ref[...], bc_ref[...]], axis=1)
    both = jnp.concatenate([cur, prev_ref[...]], axis=0)
    prev_ref[...] = cur
    acts = []
    for n in range(cur.shape[1] // CONV_COLS):
        cols = slice(n * CONV_COLS, (n + 1) * CONV_COLS)
        shifted = jnp.dot(shift_ref[...], both[:, cols], preferred_element_type=F32)
        conv = cb_ref[:, cols] + cw_ref[SSM_CONV - 1:SSM_CONV, cols] * cur[:, cols].astype(F32)
        for jj in range(SSM_CONV - 1):
            k = SSM_CONV - 1 - jj
            conv = conv + cw_ref[jj:jj + 1, cols] * shifted[(k - 1) * q:k * q, :]
        acts.append(_silu(conv))
        yield
    act = jnp.concatenate(acts, axis=1)
    xs = act[:, :d_inner]
    b16 = act[:, d_inner:d_inner + gn].astype(BF16)
    c16 = act[:, d_inner + gn:].astype(BF16)

    lane = lax.broadcasted_iota(jnp.int32, (q, LANES), 1)
    head_lane = lane % 8 == DT_LANE
    dt = jnp.where(head_lane, _softplus(s_ref[...] + dtb_ref[...]), 0.0)
    da = dt * (-jnp.exp(alog_ref[...]))
    tri = _lower_tri(q)
    cs = _tri_cumsum(tri.astype(BF16), da)
    yield
    cs2 = cs * LOG2E
    key_t = (cs2 - jnp.log2(dt)).T
    last = cs[q - 1:q, :]
    w_end = dt * jnp.exp(last - cs)
    ecs = jnp.exp(cs)
    yield

    def expand(v):
        hi, mid, _ = _split3(jnp.where(head_lane, v, 0.0))
        both = hi.astype(F32) + pltpu.roll(mid.astype(F32), 1, axis=1)
        return jnp.dot(both.astype(BF16), expand_ref[...], preferred_element_type=F32)

    w_e = expand(w_end)
    ecs_e = expand(ecs)
    xw16 = (xs * w_e).astype(BF16)
    yield

    xs16 = xs.astype(BF16)
    first_half = lax.broadcasted_iota(jnp.int32, (q, LANES), 1) < HEAD_DIM

    y_parts = []
    for g in range(SSM_GROUPS):
        bg = b16[:, g * SSM_STATE:(g + 1) * SSM_STATE]
        cg = c16[:, g * SSM_STATE:(g + 1) * SSM_STATE]
        cb = lax.dot_general(cg, bg, (((1,), (1,)), ((), ())), preferred_element_type=F32)
        st_prev = st_ref[g]
        y_off = jnp.dot(cg, st_prev.astype(BF16), preferred_element_type=F32)
        yield
        for pair in range(heads_per_group // 2):
            lo_lane = g * gw + pair * LANES
            halves = []
            for half in range(2):
                h = 8 * (g * heads_per_group + pair * 2 + half) + DT_LANE
                seg = cs2[:, h:h + 1] - key_t[h:h + 1, :]
                mmat = (cb * jnp.exp2(jnp.where(tri, seg, -jnp.inf))).astype(BF16)
                halves.append(jnp.dot(mmat, xs16[:, lo_lane:lo_lane + LANES], preferred_element_type=F32))
            y_pair = jnp.where(first_half, halves[0], halves[1])
            y_pair = y_pair + y_off[:, pair * LANES:(pair + 1) * LANES] * ecs_e[:, lo_lane:lo_lane + LANES]
            y_parts.append(y_pair)
            yield
        bg_t = bg.astype(F32).T.astype(BF16)
        new_st = jnp.dot(bg_t, xw16[:, g * gw:(g + 1) * gw], preferred_element_type=F32)
        st_ref[g] = st_prev * ecs_e[q - 1:q, g * gw:(g + 1) * gw] + new_st
        yield

    y_ref[...] = (jnp.concatenate(y_parts, axis=1) + xs * dskip_ref[...]).astype(BF16)
    yield


def _ssd_kernel(xs_ref, bc_ref, s_ref, cw_ref, cb_ref, dtb_ref, alog_ref, dskip_ref,
                expand_ref, shift_ref, y_ref, prev_ref, st_ref, *, n_heads):
    c = pl.program_id(1)
    q = SSM_CHUNK
    for cc in range(xs_ref.shape[1] // q):
        rows = slice(cc * q, (cc + 1) * q)
        is_first = (c == 0) if cc == 0 else None
        chunks = [_ssd_chunk(is_first, xs_ref.at[seq, rows], bc_ref.at[seq, rows], s_ref.at[seq, rows], cw_ref, cb_ref,
                             dtb_ref, alog_ref, dskip_ref, expand_ref, shift_ref, y_ref.at[seq, rows],
                             prev_ref.at[seq], st_ref.at[seq], n_heads)
                  for seq in range(xs_ref.shape[0])]
        for _ in zip(*chunks):
            pass


def _ssd(proj, small, conv_w, conv_b, dtb_vec, alog_vec, dskip_e, expand_m, shift_m, bsz, seqlen, d_model, n_heads):
    q = SSM_CHUNK
    conv_dim = conv_w.shape[1]
    gw = (n_heads // SSM_GROUPS) * HEAD_DIM
    nb = 2 if bsz % 2 == 0 else 1
    rows = SSD_CHUNKS_PER_STEP * q
    assert seqlen % rows == 0
    proj3 = proj.reshape(bsz, seqlen, proj.shape[1])
    const = lambda b, c: (0, 0)
    y = pl.pallas_call(
        functools.partial(_ssd_kernel, n_heads=n_heads),
        grid=(bsz // nb, seqlen // rows),
        in_specs=[
            pl.BlockSpec((nb, rows, d_model), lambda b, c: (b, c, COL_XS)),
            pl.BlockSpec((nb, rows, d_model), lambda b, c: (b, c, COL_BC)),
            pl.BlockSpec((nb, rows, LANES), lambda b, c: (b, c, 0)),
            pl.BlockSpec((SSM_CONV, conv_dim), const),
            pl.BlockSpec((1, conv_dim), const),
            pl.BlockSpec((1, LANES), const),
            pl.BlockSpec((1, LANES), const),
            pl.BlockSpec((1, d_model), const),
            pl.BlockSpec((LANES, d_model), const),
            pl.BlockSpec(((SSM_CONV - 1) * q, 2 * q), const),
        ],
        out_specs=pl.BlockSpec((nb, rows, d_model), lambda b, c: (b, c, 0)),
        out_shape=jax.ShapeDtypeStruct((bsz, seqlen, d_model), BF16),
        scratch_shapes=[
            pltpu.VMEM((nb, q, conv_dim), BF16),
            pltpu.VMEM((nb, SSM_GROUPS, SSM_STATE, gw), F32),
        ],
        compiler_params=pltpu.CompilerParams(
            dimension_semantics=("arbitrary", "arbitrary"), vmem_limit_bytes=VMEM_LIMIT),
        name="ssd_branch",
    )(proj3, proj3, small.reshape(bsz, seqlen, LANES), conv_w, conv_b, dtb_vec, alog_vec, dskip_e, expand_m, shift_m)
    return y.reshape(bsz * seqlen, d_model)


def _half_silu_gate(v16, zh16):
    return v16 * zh16 * (jnp.tanh(zh16) + 1.0)


def _merge_kernel(ys_ref, zs_ref, oa_ref, za_ref, gs_ref, ga_ref, x_ref, ws_ref, wa_ref, wo_ref, pg_ref, o_ref):
    y_att = _half_silu_gate(oa_ref[...], za_ref[...])
    p_att = jnp.dot(y_att, wa_ref[...], preferred_element_type=F32)
    yg = _half_silu_gate(ys_ref[...], zs_ref[...]).astype(F32)
    gw = yg.shape[1] // SSM_GROUPS
    normed = []
    for g in range(SSM_GROUPS):
        blk = yg[:, g * gw:(g + 1) * gw]
        normed.append(blk * lax.rsqrt(jnp.mean(blk * blk, axis=-1, keepdims=True) + RMS_EPS))
    y_ssm = jnp.concatenate(normed, axis=1).astype(BF16)
    p_ssm = jnp.dot(y_ssm, ws_ref[...], preferred_element_type=F32)
    gate_s = (jnp.tanh(gs_ref[...]) + 1.0).astype(F32)
    gate_a = (jnp.tanh(ga_ref[...]) + 1.0).astype(F32)
    merged = gate_s * p_ssm + gate_a * p_att
    out = jnp.dot(merged.astype(BF16), wo_ref[...], preferred_element_type=F32)
    ms = jnp.mean(out * out, axis=-1, keepdims=True)
    o_ref[...] = x_ref[...] + out * lax.rsqrt(ms + RMS_EPS) * pg_ref[...]


def _merge(y_scan, o_att, proj, x2, w_ssm, w_att, w_out, post_g, tm):
    m, d = x2.shape
    row = lambda i: (i, 0)
    full = lambda i: (0, 0)
    return pl.pallas_call(
        _merge_kernel,
        grid=(m // tm,),
        in_specs=[
            pl.BlockSpec((tm, d), row),
            pl.BlockSpec((tm, d), lambda i: (i, COL_Z_SSM)),
            pl.BlockSpec((tm, d), row),
            pl.BlockSpec((tm, d), lambda i: (i, COL_Z_ATT)),
            pl.BlockSpec((tm, d), lambda i: (i, COL_G_SSM)),
            pl.BlockSpec((tm, d), lambda i: (i, COL_G_ATT)),
            pl.BlockSpec((tm, d), row),
            pl.BlockSpec((d, d), full),
            pl.BlockSpec((d, d), full),
            pl.BlockSpec((d, d), full),
            pl.BlockSpec((1, d), full),
        ],
        out_specs=pl.BlockSpec((tm, d), row),
        out_shape=jax.ShapeDtypeStruct((m, d), F32),
        compiler_params=pltpu.CompilerParams(
            dimension_semantics=("arbitrary",), vmem_limit_bytes=VMEM_LIMIT),
        name="gated_merge",
    )(y_scan, proj, o_att, proj, proj, proj, x2, w_ssm, w_att, w_out, post_g)


def _head_lanes(v, slot):
    cols = [jnp.zeros_like(v, dtype=F32)] * 8
    cols[slot] = v.astype(F32)
    return jnp.stack(cols, axis=-1).reshape(1, LANES)


def _layer(x, pre_g, w_in, conv_w, conv_b, dt_bias, a_log, d_skip, ssm_norm_g, fgate_b,
           w_branch_ssm, w_branch_att, w_out, post_g):
    bsz, seqlen, d = x.shape
    n_heads = d // HEAD_DIM
    assert n_heads * 8 == LANES, "small-projection lane layout assumes 16 heads"
    gn = SSM_GROUPS * SSM_STATE
    sizes = (d, d + 2 * gn, n_heads, d, d, d, n_heads, d, d, d)
    offs = [0]
    for sz in sizes:
        offs.append(offs[-1] + sz)
    z_ssm, xbc, dt_w, q_w, k_w, v_w, f_w, z_att, g_ssm, g_att = (w_in[:, offs[n]:offs[n + 1]] for n in range(10))
    w_big = jnp.concatenate([0.5 * z_ssm, xbc, k_w, 0.5 * z_att, 0.5 * g_ssm, 0.5 * g_att], axis=1).astype(BF16)
    w_t = jnp.concatenate([(q_w * (HEAD_DIM ** -0.5 * LOG2E)).T, v_w.T], axis=0).astype(BF16)
    slots = [jnp.zeros_like(f_w)] * 8
    slots[F_LANE], slots[DT_LANE] = f_w, dt_w
    w_small = jnp.stack(slots, axis=-1).reshape(d, LANES).astype(BF16)

    x2 = x.reshape(bsz * seqlen, d)
    proj, small, q_t, vta, vtb = _in_proj(x2, pre_g.reshape(1, d), w_big, w_t, w_small, tm=min(1024, bsz * seqlen))

    gk, gq_t = _fcumsum(small, _head_lanes(fgate_b, F_LANE), bsz, seqlen)
    o_att = _attention(proj, q_t, vta, vtb, gk, gq_t, bsz, seqlen, d, tq=min(2048, seqlen))

    head_of_col = np.arange(d) // HEAD_DIM
    lane_ids = np.arange(LANES)[:, None]
    expand_m = jnp.asarray((lane_ids == 8 * head_of_col[None, :] + DT_LANE)
                           | (lane_ids == 8 * head_of_col[None, :] + DT_LANE + 1), BF16)
    t_ids = np.arange(SSM_CHUNK)[:, None]
    src = np.arange(2 * SSM_CHUNK)[None, :]
    shift_m = jnp.asarray(np.concatenate([src == np.where(t_ids >= k, t_ids - k, 2 * SSM_CHUNK + t_ids - k)
                                          for k in range(1, SSM_CONV)], axis=0), BF16)
    y_scan = _ssd(proj, small, conv_w, conv_b.reshape(1, -1), _head_lanes(dt_bias, DT_LANE), _head_lanes(a_log, DT_LANE),
                 jnp.repeat(d_skip.astype(F32), HEAD_DIM).reshape(1, d), expand_m, shift_m,
                 bsz, seqlen, d, n_heads)

    out = _merge(y_scan, o_att, proj, x2, (ssm_norm_g.astype(F32)[:, None] * w_branch_ssm).astype(BF16),
                 w_branch_att.astype(BF16), (0.5 * w_out).astype(BF16), post_g.reshape(1, d), tm=min(512, bsz * seqlen))
    return out.reshape(bsz, seqlen, d)


def kernel(x, pre_norm_g, w_in, conv_w, conv_b, dt_bias, a_log, d_skip, ssm_norm_g, fgate_b,
           w_branch_ssm, w_branch_att, w_out, post_norm_g):
    for i in range(pre_norm_g.shape[0]):
        x = _layer(x, pre_norm_g[i], w_in[i], conv_w[i], conv_b[i], dt_bias[i], a_log[i], d_skip[i],
                   ssm_norm_g[i], fgate_b[i], w_branch_ssm[i], w_branch_att[i], w_out[i], post_norm_g[i])
    return x
```
